```python
import math
import jax, jax.numpy as jnp
from jax import lax
import numpy as np


D_MODEL = 1024
BATCH = 2
SEQ = 8192
DEPTH = 2
DEC_BATCH = 32
DEC_SEQ = 1
PAST_LEN = 16384
PAGE_SIZE = 128

N_BRANCH = 4
BRANCH_WIDTH = D_MODEL // 4
IN_WIDTH = 10 * BRANCH_WIDTH
CHUNK = 128
A_HEADS = 4
A_HEAD_DIM = BRANCH_WIDTH // A_HEADS
B_CONV = 3
C_HEADS = 4
C_HEAD_DIM = BRANCH_WIDTH // C_HEADS
MOBA_BLOCK = 256
MOBA_TOPK = 3
Q_BLOCK = 128
D_CONV = 31
D_FF = 2816
N_EXPERTS = 8
TOP_K = 2
D_EXPERT = 3584
N_DENSE = (DEPTH + 1) // 2
N_MOE = DEPTH // 2
PLE_DIM = 256
ALPHA = (2 * DEPTH) ** 0.25
BETA = (8 * DEPTH) ** -0.25
LN_EPS = 1e-5

kernel_name = 'hybrid_gated_branch_decoder_step'


def layer_norm(x, g, b):
    xf = x.astype(jnp.float32)
    mu = jnp.mean(xf, axis=-1, keepdims=True)
    var = jnp.mean(jnp.square(xf - mu), axis=-1, keepdims=True)
    y = (xf - mu) * lax.rsqrt(var + LN_EPS) * g.astype(jnp.float32) + b.astype(jnp.float32)
    return y.astype(x.dtype)


def causal_dwconv(xp, w):
    c = xp.shape[-1]
    return lax.conv_general_dilated(xp, w[:, None, :].astype(xp.dtype), window_strides=(1,), padding='VALID',
                                    dimension_numbers=('NWC', 'WIO', 'NWC'), feature_group_count=c)


def chunk_spatial_mix(vn, w_s, b_s):
    n, t, _ = vn.shape
    nc = -(-t // CHUNK)
    vp = jnp.pad(vn, ((0, 0), (0, nc * CHUNK - t), (0, 0))).reshape(n, nc, CHUNK, A_HEADS, A_HEAD_DIM)
    w = jnp.tril(w_s).astype(vn.dtype)
    s = jnp.einsum('hts,ncshd->ncthd', w, vp) + b_s.T[None, None, :, :, None]
    return s.reshape(n, nc * CHUNK, BRANCH_WIDTH)[:, :t]


def moba_attention(q, k, v, pos0):
    n, tq, h, dh = q.shape
    lk = k.shape[1]
    nb = -(-lk // MOBA_BLOCK)
    pad = nb * MOBA_BLOCK - lk
    kb = jnp.pad(k, ((0, 0), (0, pad), (0, 0), (0, 0))).reshape(n, nb, MOBA_BLOCK, h, dh).transpose(0, 3, 1, 2, 4)
    vb = jnp.pad(v, ((0, 0), (0, pad), (0, 0), (0, 0))).reshape(n, nb, MOBA_BLOCK, h, dh).transpose(0, 3, 1, 2, 4)
    kmean = jnp.mean(kb.astype(jnp.float32), axis=3)
    qb = min(Q_BLOCK, tq)
    nq = -(-tq // qb)
    qp = jnp.pad(q, ((0, 0), (0, nq * qb - tq), (0, 0), (0, 0)))
    q_chunks = qp.reshape(n, nq, qb, h, dh).transpose(1, 0, 3, 2, 4)
    pos = jnp.minimum(pos0 + jnp.arange(nq * qb, dtype=jnp.int32), pos0 + tq - 1).reshape(nq, qb)
    k_sel = min(MOBA_TOPK, nb)
    n_ix = jnp.arange(n)[:, None, None, None]
    h_ix = jnp.arange(h)[None, :, None, None]
    blk_ar = jnp.arange(nb, dtype=jnp.int32)
    key_ar = jnp.arange(MOBA_BLOCK, dtype=jnp.int32)
    scale = 1.0 / math.sqrt(dh)

    def attend(args):
        qc, pc = args
        own = pc // MOBA_BLOCK
        gate = jnp.einsum('nhqd,nhbd->nhqb', qc.astype(jnp.float32), kmean)
        gate = jnp.where(blk_ar[None, :] < own[:, None], gate, -jnp.inf)
        _, sel = lax.top_k(gate, k_sel)
        sel_ok = sel < own[:, None]
        blocks = jnp.concatenate([sel, jnp.broadcast_to(own[:, None], (n, h, qb, 1))], axis=-1)
        blk_ok = jnp.concatenate([sel_ok, jnp.ones((n, h, qb, 1), bool)], axis=-1)
        kg = kb[n_ix, h_ix, blocks]
        vg = vb[n_ix, h_ix, blocks]
        s = jnp.einsum('nhqd,nhqjkd->nhqjk', qc, kg).astype(jnp.float32) * scale
        key_pos = blocks[..., None] * MOBA_BLOCK + key_ar
        ok = blk_ok[..., None] & (key_pos <= pc[:, None, None])
        s = jnp.where(ok, s, -jnp.inf)
        pr = jax.nn.softmax(s.reshape(n, h, qb, -1), axis=-1).reshape(s.shape).astype(vg.dtype)
        return jnp.einsum('nhqjk,nhqjkd->nhqd', pr, vg)

    out = lax.map(attend, (q_chunks, pos))
    return out.transpose(1, 0, 3, 2, 4).reshape(n, nq * qb, h, dh)[:, :tq]


def token_mixers(h, hist_b, hist_d, k_past, v_past, pos0, w_in, w_gate, b_gate, a_ln_g, a_ln_b, a_w_s, a_b_s,
                 b_conv_w, d_conv_w, d_conv_b, d_ln_g, d_ln_b, w_branch, w_out):
    n, t, _ = h.shape
    a_u, a_v, b_b, b_c, b_x, c_q, c_k, c_v, d_a, d_g = jnp.split(h @ w_in, 10, axis=-1)
    a_vn = layer_norm(jax.nn.gelu(a_v), a_ln_g, a_ln_b)
    y_a = jax.nn.gelu(a_u) * chunk_spatial_mix(a_vn, a_w_s, a_b_s)
    cb = jnp.concatenate([hist_b.astype(h.dtype), b_c * b_x], axis=1)
    y_b = b_b * causal_dwconv(cb, b_conv_w)
    tail_b = cb[:, -(B_CONV - 1):]
    q = c_q.reshape(n, t, C_HEADS, C_HEAD_DIM)
    k = c_k.reshape(n, t, C_HEADS, C_HEAD_DIM)
    v = c_v.reshape(n, t, C_HEADS, C_HEAD_DIM)
    k_full = jnp.concatenate([k_past.astype(h.dtype), k], axis=1)
    v_full = jnp.concatenate([v_past.astype(h.dtype), v], axis=1)
    y_c = moba_attention(q, k_full, v_full, pos0).reshape(n, t, BRANCH_WIDTH)
    cd = jnp.concatenate([hist_d.astype(h.dtype), d_a * jax.nn.sigmoid(d_g)], axis=1)
    y_d = jax.nn.silu(layer_norm(causal_dwconv(cd, d_conv_w) + d_conv_b, d_ln_g, d_ln_b))
    tail_d = cd[:, -(D_CONV - 1):]
    branches = jnp.einsum('ntjc,jcd->ntjd', jnp.stack([y_a, y_b, y_c, y_d], axis=2), w_branch)
    gates = jax.nn.sigmoid(h @ w_gate + b_gate).reshape(n, t, N_BRANCH, D_MODEL)
    out = jnp.sum(gates * branches, axis=2) @ w_out
    return out, a_vn, tail_b, tail_d, k, v


def swiglu(x, w_gate, w_up, w_down):
    return (jax.nn.silu(x @ w_gate) * (x @ w_up)) @ w_down


def moe_swiglu(x, w_router, b_router, w_gate, w_up, w_down):
    logits = (x @ w_router).astype(jnp.float32) + b_router.astype(jnp.float32)
    top_val, top_idx = lax.top_k(logits, TOP_K)
    weights = jax.nn.softmax(top_val, axis=-1)
    combine = jnp.sum(jax.nn.one_hot(top_idx, N_EXPERTS, dtype=jnp.float32) * weights[..., None], axis=-2)
    out = jnp.zeros_like(x)
    for e in range(N_EXPERTS):
        out = out + combine[..., e:e + 1].astype(x.dtype) * swiglu(x, w_gate[e], w_up[e], w_down[e])
    return out


def run_group(x, p_emb, hist_b, hist_d, paged_kv, pos0, W):
    n = x.shape[0]
    states = []
    for i in range(DEPTH):
        if hist_b is None:
            hb = jnp.zeros((n, B_CONV - 1, BRANCH_WIDTH), x.dtype)
            hd = jnp.zeros((n, D_CONV - 1, BRANCH_WIDTH), x.dtype)
            kp = jnp.zeros((n, 0, C_HEADS, C_HEAD_DIM), x.dtype)
            vp = jnp.zeros((n, 0, C_HEADS, C_HEAD_DIM), x.dtype)
        else:
            hb = hist_b[i]
            hd = hist_d[i]
            cache_k, cache_v, page_table = paged_kv
            kp = cache_k[i][page_table].reshape(n, -1, C_HEADS, C_HEAD_DIM)
            vp = cache_v[i][page_table].reshape(n, -1, C_HEADS, C_HEAD_DIM)
        mix, va, tb, td, kn, vn = token_mixers(
            x, hb, hd, kp, vp, pos0, W['w_in'][i], W['w_gate'][i], W['b_gate'][i], W['a_ln_g'][i], W['a_ln_b'][i],
            W['a_w_s'][i], W['a_b_s'][i], W['b_conv_w'][i], W['d_conv_w'][i], W['d_conv_b'][i], W['d_ln_g'][i],
            W['d_ln_b'][i], W['w_branch'][i], W['w_out'][i])
        x = layer_norm(ALPHA * x + mix, W['ln_g'][i, 0], W['ln_b'][i, 0])
        if i % 2 == 0:
            j = i // 2
            f = swiglu(x, W['ffn_w_gate'][j], W['ffn_w_up'][j], W['ffn_w_down'][j])
        else:
            j = i // 2
            f = moe_swiglu(x, W['moe_w_router'][j], W['moe_b_router'][j], W['moe_w_gate'][j],
                           W['moe_w_up'][j], W['moe_w_down'][j])
        x = layer_norm(ALPHA * x + f, W['ln_g'][i, 1], W['ln_b'][i, 1])
        ple = jax.nn.sigmoid(x @ W['ple_w_gate'][i]) * (p_emb[i] @ W['ple_w_proj'][i])
        x = layer_norm(ALPHA * x + ple, W['ln_g'][i, 2], W['ln_b'][i, 2])
        states.append((va, tb, td, kn, vn))
    va, tb, td, kn, vn = [jnp.stack(z) for z in zip(*states)]
    return x, va, tb, td, kn, vn


def setup_inputs(seed: int = 0) -> dict:
    key = jax.random.key(seed)
    keys = iter(jax.random.split(key, 48))

    def normal(shape, scale):
        return jax.random.normal(next(keys), shape, jnp.float32) * scale

    n_pages = PAST_LEN // PAGE_SIZE
    n_used = DEC_BATCH * n_pages
    n_pool = n_used + max(1, n_used // 4)
    bw = BRANCH_WIDTH
    d = D_MODEL
    return {
        'x_prompt': normal((BATCH, SEQ, d), 1.0),
        'x_sample': normal((DEC_BATCH, DEC_SEQ, d), 1.0),
        'p_prompt': normal((DEPTH, BATCH, SEQ, PLE_DIM), 1.0),
        'p_sample': normal((DEPTH, DEC_BATCH, DEC_SEQ, PLE_DIM), 1.0),
        'cache_k': normal((DEPTH, n_pool, PAGE_SIZE, C_HEADS, C_HEAD_DIM), 1.0),
        'cache_v': normal((DEPTH, n_pool, PAGE_SIZE, C_HEADS, C_HEAD_DIM), 1.0),
        'state_conv_b': normal((DEPTH, DEC_BATCH, B_CONV - 1, bw), 1.0),
        'state_conv_d': normal((DEPTH, DEC_BATCH, D_CONV - 1, bw), 1.0),
        'page_table': jax.random.permutation(next(keys), n_pool)[:n_used].reshape(DEC_BATCH, n_pages).astype(jnp.int32),
        'w_in': normal((DEPTH, d, IN_WIDTH), d ** -0.5),
        'w_gate': normal((DEPTH, d, N_BRANCH * d), d ** -0.5),
        'b_gate': normal((DEPTH, N_BRANCH * d), 0.02),
        'a_ln_g': 1.0 + normal((DEPTH, bw), 0.02),
        'a_ln_b': normal((DEPTH, bw), 0.02),
        'a_w_s': normal((DEPTH, A_HEADS, CHUNK, CHUNK), CHUNK ** -0.5),
        'a_b_s': 1.0 + normal((DEPTH, A_HEADS, CHUNK), 0.1),
        'b_conv_w': normal((DEPTH, B_CONV, bw), B_CONV ** -0.5),
        'd_conv_w': normal((DEPTH, D_CONV, bw), D_CONV ** -0.5),
        'd_conv_b': normal((DEPTH, bw), 0.02),
        'd_ln_g': 1.0 + normal((DEPTH, bw), 0.02),
        'd_ln_b': normal((DEPTH, bw), 0.02),
        'w_branch': normal((DEPTH, N_BRANCH, bw, d), bw ** -0.5 * BETA),
        'w_out': normal((DEPTH, d, d), d ** -0.5 * BETA),
        'ln_g': 1.0 + normal((DEPTH, 3, d), 0.02),
        'ln_b': normal((DEPTH, 3, d), 0.02),
        'ffn_w_gate': normal((N_DENSE, d, D_FF), d ** -0.5),
        'ffn_w_up': normal((N_DENSE, d, D_FF), d ** -0.5),
        'ffn_w_down': normal((N_DENSE, D_FF, d), D_FF ** -0.5 * BETA),
        'moe_w_router': normal((N_MOE, d, N_EXPERTS), d ** -0.5),
        'moe_b_router': normal((N_MOE, N_EXPERTS), 0.01),
        'moe_w_gate': normal((N_MOE, N_EXPERTS, d, D_EXPERT), d ** -0.5),
        'moe_w_up': normal((N_MOE, N_EXPERTS, d, D_EXPERT), d ** -0.5),
        'moe_w_down': normal((N_MOE, N_EXPERTS, D_EXPERT, d), D_EXPERT ** -0.5 * BETA),
        'ple_w_gate': normal((DEPTH, d, d), d ** -0.5),
        'ple_w_proj': normal((DEPTH, PLE_DIM, d), PLE_DIM ** -0.5 * BETA),
    }


def reference(x_prompt, x_sample, p_prompt, p_sample, cache_k, cache_v, state_conv_b, state_conv_d, page_table,
              w_in, w_gate, b_gate, a_ln_g, a_ln_b, a_w_s, a_b_s, b_conv_w, d_conv_w, d_conv_b, d_ln_g, d_ln_b,
              w_branch, w_out, ln_g, ln_b, ffn_w_gate, ffn_w_up, ffn_w_down, moe_w_router, moe_b_router,
              moe_w_gate, moe_w_up, moe_w_down, ple_w_gate, ple_w_proj):
    W = dict(w_in=w_in, w_gate=w_gate, b_gate=b_gate, a_ln_g=a_ln_g, a_ln_b=a_ln_b, a_w_s=a_w_s, a_b_s=a_b_s,
             b_conv_w=b_conv_w, d_conv_w=d_conv_w, d_conv_b=d_conv_b, d_ln_g=d_ln_g, d_ln_b=d_ln_b,
             w_branch=w_branch, w_out=w_out, ln_g=ln_g, ln_b=ln_b, ffn_w_gate=ffn_w_gate, ffn_w_up=ffn_w_up,
             ffn_w_down=ffn_w_down, moe_w_router=moe_w_router, moe_b_router=moe_b_router, moe_w_gate=moe_w_gate,
             moe_w_up=moe_w_up, moe_w_down=moe_w_down, ple_w_gate=ple_w_gate, ple_w_proj=ple_w_proj)
    y_prompt, _, conv_b_prompt, conv_d_prompt, k_prompt, v_prompt = run_group(
        x_prompt, p_prompt, None, None, None, 0, W)
    past_len = page_table.shape[1] * cache_k.shape[2]
    y_sample, chunk_v_sample, conv_b_sample, conv_d_sample, k_sample, v_sample = run_group(
        x_sample, p_sample, state_conv_b, state_conv_d, (cache_k, cache_v, page_table), past_len, W)
    return (y_prompt, y_sample, k_prompt, v_prompt, k_sample, v_sample, conv_b_prompt, conv_b_sample,
            conv_d_prompt, conv_d_sample, chunk_v_sample)
```

```python
import functools
import math

import jax
import jax.numpy as jnp
from jax import lax
from jax.experimental import pallas as pl
from jax.experimental.pallas import tpu as pltpu

F32 = jnp.float32
BF16 = jnp.bfloat16
HIGHEST = lax.Precision.HIGHEST

BW = 256
N_BRANCH = 4
CHUNK = 128
HEADS = 4
HEAD_DIM = BW // HEADS
HEAD_SHIFT = 6
MOBA_BLOCK = 256
MOBA_TOPK = 3
B_CONV = 3
D_CONV = 31
N_EXPERTS = 8
LN_EPS = 1e-5
NEG = -1e30
LANES = 128
SUBLANES = 8
V7X_VMEM_LIMIT = 56 * 1024 * 1024
CONV_ROWS = 64
B_HALO = 8
D_HALO = 32


def _params(*sem):
    return pltpu.CompilerParams(dimension_semantics=sem, vmem_limit_bytes=V7X_VMEM_LIMIT)


def _ln(x, g, b):
    mu = jnp.mean(x, axis=-1, keepdims=True)
    xc = x - mu
    var = jnp.mean(xc * xc, axis=-1, keepdims=True)
    return xc * lax.rsqrt(var + LN_EPS) * g + b


def _gelu(x):
    c = math.sqrt(2.0 / math.pi)
    return x * (0.5 * (1.0 + jnp.tanh(c * (x + 0.044715 * (x * x * x)))))


def _sigmoid(x):
    return 1.0 / (1.0 + jnp.exp(-x))


def _silu(x):
    return x * _sigmoid(x)


def _mm(a, b):
    return jnp.dot(a, b, preferred_element_type=F32)


def _mm_nt(a, b, **kw):
    return lax.dot_general(a, b, (((1,), (1,)), ((), ())), preferred_element_type=F32, **kw)


def _lane_head(shape):
    return lax.shift_right_logical(lax.broadcasted_iota(jnp.int32, shape, len(shape) - 1), HEAD_SHIFT)


def _row_tile(m, want):
    tm = min(want, m)
    assert m % tm == 0, (m, tm)
    return tm


def _inmix_prompt_kernel(x_ref, w_ref, alng_ref, alnb_ref, aws_ref, abias_ref, bcw_ref, dcw_ref, dcb_ref,
                         dlng_ref, dlnb_ref,
                         ya_ref, yb_ref, yd_ref, q_ref, k_ref, v_ref, kb_ref, vt_ref, km_ref, tb_ref, td_ref,
                         cb_buf, cd_buf, *, tm, scale):
    t = pl.program_id(1)
    last = pl.num_programs(1) - 1

    @pl.when(t == 0)
    def _():
        cb_buf[0:B_HALO, :] = jnp.zeros((B_HALO, BW), F32)
        cd_buf[0:D_HALO, :] = jnp.zeros((D_HALO, BW), F32)

    xb = x_ref[...].astype(BF16)

    def proj(j):
        return _mm(xb, w_ref[:, j * BW:(j + 1) * BW])

    gu = _gelu(proj(0))
    vnb = _ln(_gelu(proj(1)), alng_ref[...], alnb_ref[...]).astype(BF16)
    row = lax.broadcasted_iota(jnp.int32, (CHUNK, CHUNK), 0)
    col = lax.broadcasted_iota(jnp.int32, (CHUNK, CHUNK), 1)
    wts = [jnp.where(row >= col, aws_ref[h], 0.0).astype(BF16) for h in range(HEADS)]
    lane_head = _lane_head((CHUNK, BW))
    for c in range(tm // CHUNK):
        sl = slice(c * CHUNK, (c + 1) * CHUNK)
        vc = vnb[sl, :]
        s = abias_ref[...]
        for h in range(HEADS):
            s = s + _mm(wts[h], jnp.where(lane_head == h, vc, jnp.zeros_like(vc)))
        ya_ref[sl, :] = (gu[sl, :] * s).astype(BF16)

    bb = proj(2)
    cb_buf[B_HALO:B_HALO + tm, :] = proj(3) * proj(4)
    off_b = B_HALO - (B_CONV - 1)
    for r0 in range(0, tm, CONV_ROWS):
        acc = jnp.zeros((CONV_ROWS, BW), F32)
        for kk in range(B_CONV):
            acc = acc + bcw_ref[kk:kk + 1, :] * cb_buf[r0 + kk + off_b:r0 + kk + off_b + CONV_ROWS, :]
        yb_ref[r0:r0 + CONV_ROWS, :] = (bb[r0:r0 + CONV_ROWS, :] * acc).astype(BF16)

    @pl.when(t == last)
    def _():
        tb_ref[...] = cb_buf[tm + off_b:tm + B_HALO, :]

    cb_buf[0:B_HALO, :] = cb_buf[tm:tm + B_HALO, :]

    q_ref[...] = proj(5) * scale
    k = proj(6)
    v = proj(7)
    k_ref[...] = k
    v_ref[...] = v
    for i in range(tm // MOBA_BLOCK):
        sl = slice(i * MOBA_BLOCK, (i + 1) * MOBA_BLOCK)
        kblk = k[sl, :]
        kb_ref[i] = kblk.astype(BF16)
        vt_ref[i] = v[sl, :].T.astype(BF16)
        km_ref[i] = jnp.sum(kblk, axis=0, keepdims=True) * (1.0 / MOBA_BLOCK)

    cd_buf[D_HALO:D_HALO + tm, :] = proj(8) * _sigmoid(proj(9))
    off_d = D_HALO - (D_CONV - 1)
    for r0 in range(0, tm, CONV_ROWS):
        acc = jnp.zeros((CONV_ROWS, BW), F32) + dcb_ref[...]
        for kk in range(D_CONV):
            acc = acc + dcw_ref[kk:kk + 1, :] * cd_buf[r0 + kk + off_d:r0 + kk + off_d + CONV_ROWS, :]
        yd_ref[r0:r0 + CONV_ROWS, :] = _silu(_ln(acc, dlng_ref[...], dlnb_ref[...])).astype(BF16)

    @pl.when(t == last)
    def _():
        td_ref[...] = cd_buf[tm + off_d:tm + D_HALO, :]

    cd_buf[0:D_HALO, :] = cd_buf[tm:tm + D_HALO, :]


def _inmix_prompt(x, w_in, alng, alnb, aws, abias, bcw, dcw, dcb, dlng, dlnb):
    b, t, d = x.shape
    tm = _row_tile(t, 512)
    assert tm % MOBA_BLOCK == 0 and tm % CONV_ROWS == 0
    nbt = tm // MOBA_BLOCK
    nb = t // MOBA_BLOCK
    full = lambda shape: pl.BlockSpec(shape, lambda i, j: (0,) * len(shape))
    rows = lambda: pl.BlockSpec((None, tm, BW), lambda i, j: (i, j, 0))
    blocks = lambda: pl.BlockSpec((None, nbt, MOBA_BLOCK, MOBA_BLOCK), lambda i, j: (i, j, 0, 0))
    out_shape = (
        jax.ShapeDtypeStruct((b, t, BW), BF16),
        jax.ShapeDtypeStruct((b, t, BW), BF16),
        jax.ShapeDtypeStruct((b, t, BW), BF16),
        jax.ShapeDtypeStruct((b, t, BW), F32),
        jax.ShapeDtypeStruct((b, t, BW), F32),
        jax.ShapeDtypeStruct((b, t, BW), F32),
        jax.ShapeDtypeStruct((b, nb, MOBA_BLOCK, BW), BF16),
        jax.ShapeDtypeStruct((b, nb, BW, MOBA_BLOCK), BF16),
        jax.ShapeDtypeStruct((b, nb, 1, BW), F32),
        jax.ShapeDtypeStruct((b, B_CONV - 1, BW), F32),
        jax.ShapeDtypeStruct((b, D_CONV - 1, BW), F32),
    )
    out_specs = (
        rows(), rows(), rows(), rows(), rows(), rows(), blocks(), blocks(),
        pl.BlockSpec((None, nbt, 1, BW), lambda i, j: (i, j, 0, 0)),
        pl.BlockSpec((None, B_CONV - 1, BW), lambda i, j: (i, 0, 0)),
        pl.BlockSpec((None, D_CONV - 1, BW), lambda i, j: (i, 0, 0)),
    )
    in_specs = [
        pl.BlockSpec((None, tm, d), lambda i, j: (i, j, 0)),
        full(w_in.shape), full(alng.shape), full(alnb.shape), full(aws.shape), full(abias.shape),
        full(bcw.shape), full(dcw.shape), full(dcb.shape), full(dlng.shape), full(dlnb.shape),
    ]
    return pl.pallas_call(
        functools.partial(_inmix_prompt_kernel, tm=tm, scale=1.0 / math.sqrt(HEAD_DIM)),
        grid=(b, t // tm), in_specs=in_specs, out_specs=out_specs, out_shape=out_shape,
        scratch_shapes=[pltpu.VMEM((B_HALO + tm, BW), F32), pltpu.VMEM((D_HALO + tm, BW), F32)],
        compiler_params=_params("arbitrary", "arbitrary"), name="inmix_prompt",
    )(x, w_in, alng, alnb, aws, abias, bcw, dcw, dcb, dlng, dlnb)


def _inmix_sample_kernel(x_ref, w_ref, alng_ref, alnb_ref, w00_ref, b0_ref, bcw_ref, hb_ref, dcw_ref, dcb_ref,
                         hd_ref, dlng_ref, dlnb_ref,
                         ya_ref, yb_ref, yd_ref, q_ref, k_ref, v_ref, vn_ref, cbn_ref, cdn_ref, *, scale):
    xb = x_ref[...].astype(BF16)

    def proj(j):
        return _mm(xb, w_ref[:, j * BW:(j + 1) * BW])

    vn = _ln(_gelu(proj(1)), alng_ref[...], alnb_ref[...])
    vn_ref[...] = vn
    ya_ref[...] = (_gelu(proj(0)) * (vn * w00_ref[...] + b0_ref[...])).astype(BF16)

    cbn = proj(3) * proj(4)
    cbn_ref[...] = cbn
    conv = bcw_ref[B_CONV - 1:B_CONV, :] * cbn
    for kk in range(B_CONV - 1):
        conv = conv + bcw_ref[kk:kk + 1, :] * hb_ref[kk]
    yb_ref[...] = (proj(2) * conv).astype(BF16)

    q_ref[...] = proj(5) * scale
    k_ref[...] = proj(6)
    v_ref[...] = proj(7)

    cdn = proj(8) * _sigmoid(proj(9))
    cdn_ref[...] = cdn
    conv = dcb_ref[...] + dcw_ref[D_CONV - 1:D_CONV, :] * cdn
    for kk in range(D_CONV - 1):
        conv = conv + dcw_ref[kk:kk + 1, :] * hd_ref[kk]
    yd_ref[...] = _silu(_ln(conv, dlng_ref[...], dlnb_ref[...])).astype(BF16)


def _inmix_sample(x, w_in, alng, alnb, w00, b0, bcw, hb, dcw, dcb, hd, dlng, dlnb):
    s = x.shape[0]
    args = (x, w_in, alng, alnb, w00, b0, bcw, hb, dcw, dcb, hd, dlng, dlnb)
    full = lambda shape: pl.BlockSpec(shape, lambda i: (0,) * len(shape))
    row = lambda dt: jax.ShapeDtypeStruct((s, BW), dt)
    out_shape = (row(BF16), row(BF16), row(BF16), row(F32), row(F32), row(F32), row(F32), row(F32), row(F32))
    return pl.pallas_call(
        functools.partial(_inmix_sample_kernel, scale=1.0 / math.sqrt(HEAD_DIM)),
        grid=(1,), in_specs=[full(a.shape) for a in args], out_specs=tuple(full((s, BW)) for _ in out_shape),
        out_shape=out_shape, compiler_params=_params("arbitrary"), name="inmix_sample",
    )(*args)


def _attn_prompt_kernel(q_ref, kb_ref, vt_ref, km_ref, yc_ref, bias_scr, out_scr, *, nb):
    c = pl.program_id(1)
    qb = MOBA_BLOCK
    q = q_ref[...]
    q16 = q.astype(BF16)
    km = km_ref[...]
    lane_head_q = _lane_head((qb, BW))
    lane_head_km = _lane_head((nb, BW))
    blk = lax.broadcasted_iota(jnp.int32, (nb, qb), 0)
    key_pos = lax.broadcasted_iota(jnp.int32, (qb, qb), 0)
    q_pos = lax.broadcasted_iota(jnp.int32, (qb, qb), 1)
    valid = blk < c

    for h in range(HEADS):
        gate = _mm_nt(jnp.where(lane_head_km == h, km, 0.0), q, precision=HIGHEST)
        g = jnp.where(valid, gate, -jnp.inf)
        sel = jnp.zeros((nb, qb), F32)
        for _ in range(min(MOBA_TOPK, nb)):
            m = jnp.max(g, axis=0, keepdims=True)
            idx = jnp.min(jnp.where(g == m, blk, nb), axis=0, keepdims=True)
            hit = blk == idx
            sel = jnp.where(hit, 1.0, sel)
            g = jnp.where(hit, -jnp.inf, g)
        bias_scr[h] = jnp.where(jnp.logical_and(sel > 0.5, valid), 0.0, NEG)

        qh = jnp.where(lane_head_q == h, q16, jnp.zeros_like(q16))
        hs = slice(h * HEAD_DIM, (h + 1) * HEAD_DIM)

        s = jnp.where(key_pos <= q_pos, _mm_nt(kb_ref[c], qh), NEG)
        m = jnp.max(s, axis=0, keepdims=True)
        p = jnp.exp(s - m)
        l = jnp.sum(p, axis=0, keepdims=True)
        acc = _mm(vt_ref[c, hs, :], p.astype(BF16))

        def body(j, carry):
            m, l, acc = carry
            s = _mm_nt(kb_ref[j], qh) + bias_scr[h, pl.ds(j, 1), :]
            m_new = jnp.maximum(m, jnp.max(s, axis=0, keepdims=True))
            a = jnp.exp(m - m_new)
            p = jnp.exp(s - m_new)
            l = a * l + jnp.sum(p, axis=0, keepdims=True)
            acc = a * acc + _mm(vt_ref[j, hs, :], p.astype(BF16))
            return m_new, l, acc

        m, l, acc = lax.fori_loop(0, c, body, (m, l, acc))
        out_scr[hs, :] = acc / l

    yc_ref[...] = out_scr[...].T.astype(BF16)


def _attn_prompt(q, kb, vt, km):
    b, t, _ = q.shape
    nb = t // MOBA_BLOCK
    return pl.pallas_call(
        functools.partial(_attn_prompt_kernel, nb=nb),
        grid=(b, nb),
        in_specs=[
            pl.BlockSpec((None, MOBA_BLOCK, BW), lambda i, j: (i, j, 0)),
            pl.BlockSpec((None, nb, MOBA_BLOCK, BW), lambda i, j: (i, 0, 0, 0)),
            pl.BlockSpec((None, nb, BW, MOBA_BLOCK), lambda i, j: (i, 0, 0, 0)),
            pl.BlockSpec((None, nb, BW), lambda i, j: (i, 0, 0)),
        ],
        out_specs=pl.BlockSpec((None, MOBA_BLOCK, BW), lambda i, j: (i, j, 0)),
        out_shape=jax.ShapeDtypeStruct((b, t, BW), BF16),
        scratch_shapes=[pltpu.VMEM((HEADS, nb, MOBA_BLOCK), F32), pltpu.VMEM((BW, MOBA_BLOCK), F32)],
        compiler_params=_params("arbitrary", "arbitrary"), name="attn_prompt",
    )(q, kb, vt, km)


def _pagesum_kernel(c_ref, o_ref, *, pp):
    for p in range(pp):
        o_ref[p:p + 1, :] = jnp.sum(c_ref[p], axis=0, keepdims=True)


def _pagesum(cache):
    n_pool, page, _ = cache.shape
    pp = next(c for c in (64, 40, 32, 16, 8) if n_pool % c == 0)
    return pl.pallas_call(
        functools.partial(_pagesum_kernel, pp=pp),
        grid=(n_pool // pp,),
        in_specs=[pl.BlockSpec((pp, page, BW), lambda i: (i, 0, 0))],
        out_specs=pl.BlockSpec((pp, BW), lambda i: (i, 0)),
        out_shape=jax.ShapeDtypeStruct((n_pool, BW), F32),
        compiler_params=_params("arbitrary"), name="pagesum",
    )(cache)


def _select_sample_kernel(pt_ref, q_ref, ps_ref, o_ref, km_scr, *, n_pages, nbp, ppb):
    b = pl.program_id(0)
    for j in range(nbp):
        tot = ps_ref[pl.ds(pt_ref[b * n_pages + ppb * j], 1), :]
        for u in range(1, ppb):
            tot = tot + ps_ref[pl.ds(pt_ref[b * n_pages + ppb * j + u], 1), :]
        km_scr[j:j + 1, :] = tot * (1.0 / MOBA_BLOCK)
    prod = km_scr[...] * q_ref[pl.ds(b, 1), :]
    row_head = lax.shift_right_logical(lax.broadcasted_iota(jnp.int32, (BW, LANES), 0), HEAD_SHIFT)
    seg = (row_head == lax.broadcasted_iota(jnp.int32, (BW, LANES), 1)).astype(F32)
    g = jnp.dot(prod, seg, precision=HIGHEST, preferred_element_type=F32)
    blk = lax.broadcasted_iota(jnp.int32, (nbp, LANES), 0)
    o_ref[...] = jnp.zeros((SUBLANES, LANES), jnp.int32)
    for r in range(MOBA_TOPK):
        m = jnp.max(g, axis=0, keepdims=True)
        idx = jnp.min(jnp.where(g == m, blk, nbp), axis=0, keepdims=True)
        o_ref[r:r + 1, :] = idx
        g = jnp.where(blk == idx, -jnp.inf, g)


def _select_sample(pt_flat, q, psum, *, n_pages, nbp, ppb):
    s = q.shape[0]
    grid_spec = pltpu.PrefetchScalarGridSpec(
        num_scalar_prefetch=1, grid=(s,),
        in_specs=[pl.BlockSpec(q.shape, lambda i, pt: (0, 0)), pl.BlockSpec(psum.shape, lambda i, pt: (0, 0))],
        out_specs=pl.BlockSpec((None, SUBLANES, LANES), lambda i, pt: (i, 0, 0)),
        scratch_shapes=[pltpu.VMEM((nbp, BW), F32)],
    )
    return pl.pallas_call(
        functools.partial(_select_sample_kernel, n_pages=n_pages, nbp=nbp, ppb=ppb),
        grid_spec=grid_spec, out_shape=jax.ShapeDtypeStruct((s, SUBLANES, LANES), jnp.int32),
        compiler_params=_params("arbitrary"), name="select_sample",
    )(pt_flat, q, psum)


def _attn_sample_kernel(pt_ref, sel_ref, ck_ref, cv_ref, q_ref, kn_ref, vn_ref, o_ref, kbuf, vbuf, sem,
                        *, n_pages, ppb):
    b = pl.program_id(0)
    per_head = MOBA_TOPK * ppb
    copies = []
    for h in range(HEADS):
        for r in range(MOBA_TOPK):
            blk = sel_ref[b * (MOBA_TOPK * HEADS) + r * HEADS + h]
            for u in range(ppb):
                i = h * per_head + r * ppb + u
                page = pt_ref[b * n_pages + blk * ppb + u]
                copies.append(pltpu.make_async_copy(ck_ref.at[page], kbuf.at[i], sem.at[0, i]))
                copies.append(pltpu.make_async_copy(cv_ref.at[page], vbuf.at[i], sem.at[1, i]))
    for cp in copies:
        cp.start()
    for cp in copies:
        cp.wait()

    q = q_ref[pl.ds(b, 1), :]
    kn = kn_ref[pl.ds(b, 1), :]
    vn = vn_ref[pl.ds(b, 1), :]
    lane_head = _lane_head((1, BW))
    out = jnp.zeros((1, BW), F32)
    for h in range(HEADS):
        hm = lane_head == h
        s_new = jnp.sum(jnp.where(hm, q * kn, 0.0), axis=1, keepdims=True)
        pages = range(h * per_head, (h + 1) * per_head)
        ss = [jnp.sum(jnp.where(hm, kbuf[i] * q, 0.0), axis=1, keepdims=True) for i in pages]
        m = s_new
        for s in ss:
            m = jnp.maximum(m, jnp.max(s, axis=0, keepdims=True))
        p_new = jnp.exp(s_new - m)
        l = p_new
        o = p_new * vn
        for i, s in zip(pages, ss):
            p = jnp.exp(s - m)
            l = l + jnp.sum(p, axis=0, keepdims=True)
            o = o + jnp.sum(p * vbuf[i], axis=0, keepdims=True)
        out = out + jnp.where(hm, o / l, 0.0)
    o_ref[...] = out


def _attn_sample(pt_flat, sel_flat, cache_k, cache_v, q, kn, vn, *, n_pages, ppb):
    s = q.shape[0]
    page = cache_k.shape[1]
    n_sel = HEADS * MOBA_TOPK * ppb
    full = lambda a: pl.BlockSpec(a.shape, lambda i, pt, sel: (0, 0))
    grid_spec = pltpu.PrefetchScalarGridSpec(
        num_scalar_prefetch=2, grid=(s,),
        in_specs=[pl.BlockSpec(memory_space=pl.ANY), pl.BlockSpec(memory_space=pl.ANY), full(q), full(kn), full(vn)],
        out_specs=pl.BlockSpec((None, 1, BW), lambda i, pt, sel: (i, 0, 0)),
        scratch_shapes=[pltpu.VMEM((n_sel, page, BW), F32), pltpu.VMEM((n_sel, page, BW), F32),
                        pltpu.SemaphoreType.DMA((2, n_sel))],
    )
    return pl.pallas_call(
        functools.partial(_attn_sample_kernel, n_pages=n_pages, ppb=ppb),
        grid_spec=grid_spec, out_shape=jax.ShapeDtypeStruct((s, 1, BW), F32),
        compiler_params=_params("arbitrary"), name="attn_sample",
    )(pt_flat, sel_flat, cache_k, cache_v, q, kn, vn)


def _merge_kernel(x_ref, ya_ref, yb_ref, yc_ref, yd_ref, wg_ref, bg_ref, wbr_ref, wo_ref, g_ref, b_ref, o_ref,
                  *, alpha):
    x = x_ref[...]
    xb = x.astype(BF16)
    d = x.shape[1]
    acc = None
    for j, y_ref in enumerate((ya_ref, yb_ref, yc_ref, yd_ref)):
        gate = _sigmoid(_mm(xb, wg_ref[:, j * d:(j + 1) * d]) + bg_ref[:, j * d:(j + 1) * d])
        term = gate * _mm(y_ref[...], wbr_ref[j])
        acc = term if acc is None else acc + term
    out = _mm(acc.astype(BF16), wo_ref[...])
    o_ref[...] = _ln(alpha * x + out, g_ref[...], b_ref[...])


def _merge(x, ya, yb, yc, yd, wg, bg, wbr, wo, g, b, *, alpha):
    m, d = x.shape
    tm = _row_tile(m, 256)
    full = lambda a: pl.BlockSpec(a.shape, lambda i: (0,) * a.ndim)
    rows = lambda w: pl.BlockSpec((tm, w), lambda i: (i, 0))
    return pl.pallas_call(
        functools.partial(_merge_kernel, alpha=alpha),
        grid=(m // tm,),
        in_specs=[rows(d), rows(BW), rows(BW), rows(BW), rows(BW), full(wg), full(bg), full(wbr), full(wo),
                  full(g), full(b)],
        out_specs=rows(d), out_shape=jax.ShapeDtypeStruct((m, d), F32),
        compiler_params=_params("arbitrary"), name="merge",
    )(x, ya, yb, yc, yd, wg, bg, wbr, wo, g, b)


def _ffn_kernel(x_ref, wg_ref, wu_ref, wd_ref, g_ref, b_ref, o_ref, acc_ref, *, alpha):
    f = pl.program_id(1)

    @pl.when(f == 0)
    def _():
        acc_ref[...] = jnp.zeros_like(acc_ref)

    xb = x_ref[...].astype(BF16)
    h = _silu(_mm(xb, wg_ref[...])) * _mm(xb, wu_ref[...])
    acc_ref[...] += _mm(h.astype(BF16), wd_ref[...])

    @pl.when(f == pl.num_programs(1) - 1)
    def _():
        o_ref[...] = _ln(alpha * x_ref[...] + acc_ref[...], g_ref[...], b_ref[...])


def _ffn_tile(dff):
    return next(c for c in (1408, 1024, 896, 768, 512, 384, 256, 128) if dff % c == 0)


def _ffn(x, wg, wu, wd, g, b, *, alpha):
    m, d = x.shape
    dff = wg.shape[1]
    tm = _row_tile(m, 512)
    tf = _ffn_tile(dff)
    full = lambda a: pl.BlockSpec(a.shape, lambda i, f: (0,) * a.ndim)
    return pl.pallas_call(
        functools.partial(_ffn_kernel, alpha=alpha),
        grid=(m // tm, dff // tf),
        in_specs=[pl.BlockSpec((tm, d), lambda i, f: (i, 0)), pl.BlockSpec((d, tf), lambda i, f: (0, f)),
                  pl.BlockSpec((d, tf), lambda i, f: (0, f)), pl.BlockSpec((tf, d), lambda i, f: (f, 0)),
                  full(g), full(b)],
        out_specs=pl.BlockSpec((tm, d), lambda i, f: (i, 0)), out_shape=jax.ShapeDtypeStruct((m, d), F32),
        scratch_shapes=[pltpu.VMEM((tm, d), F32)],
        compiler_params=_params("arbitrary", "arbitrary"), name="ffn",
    )(x, wg, wu, wd, g, b)


def _router_kernel(x_ref, wr_ref, br_ref, o_ref):
    lg = jnp.dot(x_ref[...], wr_ref[...], precision=HIGHEST, preferred_element_type=F32) + br_ref[...]
    lane = lax.broadcasted_iota(jnp.int32, lg.shape, 1)
    m1 = jnp.max(lg, axis=1, keepdims=True)
    hit1 = lane == jnp.min(jnp.where(lg == m1, lane, LANES), axis=1, keepdims=True)
    lg2 = jnp.where(hit1, -jnp.inf, lg)
    m2 = jnp.max(lg2, axis=1, keepdims=True)
    hit2 = lane == jnp.min(jnp.where(lg2 == m2, lane, LANES), axis=1, keepdims=True)
    e = jnp.exp(m2 - m1)
    o_ref[...] = jnp.where(hit1, 1.0 / (1.0 + e), 0.0) + jnp.where(hit2, e / (1.0 + e), 0.0)


def _router(x, wr, br):
    m, d = x.shape
    tm = _row_tile(m, 512)
    full = lambda a: pl.BlockSpec(a.shape, lambda i: (0,) * a.ndim)
    return pl.pallas_call(
        _router_kernel, grid=(m // tm,),
        in_specs=[pl.BlockSpec((tm, d), lambda i: (i, 0)), full(wr), full(br)],
        out_specs=pl.BlockSpec((tm, LANES), lambda i: (i, 0)), out_shape=jax.ShapeDtypeStruct((m, LANES), F32),
        compiler_params=_params("arbitrary"), name="router",
    )(x, wr, br)


def _moe_kernel(x_ref, c_ref, wg_ref, wu_ref, wd_ref, g_ref, b_ref, o_ref, acc_ref, *, alpha):
    e = pl.program_id(1)
    f = pl.program_id(2)

    @pl.when(jnp.logical_and(e == 0, f == 0))
    def _():
        acc_ref[...] = jnp.zeros_like(acc_ref)

    comb = c_ref[...]
    lane = lax.broadcasted_iota(jnp.int32, comb.shape, 1)
    ce = jnp.sum(jnp.where(lane == e, comb, 0.0), axis=1, keepdims=True)
    xb = x_ref[...].astype(BF16)
    h = _silu(_mm(xb, wg_ref[...])) * _mm(xb, wu_ref[...])
    acc_ref[...] += ce * _mm(h.astype(BF16), wd_ref[...])

    @pl.when(jnp.logical_and(e == pl.num_programs(1) - 1, f == pl.num_programs(2) - 1))
    def _():
        o_ref[...] = _ln(alpha * x_ref[...] + acc_ref[...], g_ref[...], b_ref[...])


def _moe(x, comb, wg, wu, wd, g, b, *, alpha):
    m, d = x.shape
    ne, _, dff = wg.shape
    tm = _row_tile(m, 512)
    tf = _ffn_tile(dff)
    full = lambda a: pl.BlockSpec(a.shape, lambda i, e, f: (0,) * a.ndim)
    return pl.pallas_call(
        functools.partial(_moe_kernel, alpha=alpha),
        grid=(m // tm, ne, dff // tf),
        in_specs=[pl.BlockSpec((tm, d), lambda i, e, f: (i, 0)), pl.BlockSpec((tm, LANES), lambda i, e, f: (i, 0)),
                  pl.BlockSpec((None, d, tf), lambda i, e, f: (e, 0, f)),
                  pl.BlockSpec((None, d, tf), lambda i, e, f: (e, 0, f)),
                  pl.BlockSpec((None, tf, d), lambda i, e, f: (e, f, 0)), full(g), full(b)],
        out_specs=pl.BlockSpec((tm, d), lambda i, e, f: (i, 0)), out_shape=jax.ShapeDtypeStruct((m, d), F32),
        scratch_shapes=[pltpu.VMEM((tm, d), F32)],
        compiler_params=_params("arbitrary", "arbitrary", "arbitrary"), name="moe",
    )(x, comb, wg, wu, wd, g, b)


def _ple_kernel(x_ref, p_ref, wg_ref, wp_ref, g_ref, b_ref, o_ref, *, alpha):
    x = x_ref[...]
    gate = _sigmoid(_mm(x.astype(BF16), wg_ref[...]))
    o_ref[...] = _ln(alpha * x + gate * _mm(p_ref[...].astype(BF16), wp_ref[...]), g_ref[...], b_ref[...])


def _ple(x, p, wg, wp, g, b, *, alpha):
    m, d = x.shape
    tm = _row_tile(m, 512)
    full = lambda a: pl.BlockSpec(a.shape, lambda i: (0,) * a.ndim)
    return pl.pallas_call(
        functools.partial(_ple_kernel, alpha=alpha),
        grid=(m // tm,),
        in_specs=[pl.BlockSpec((tm, d), lambda i: (i, 0)), pl.BlockSpec((tm, p.shape[1]), lambda i: (i, 0)),
                  full(wg), full(wp), full(g), full(b)],
        out_specs=pl.BlockSpec((tm, d), lambda i: (i, 0)), out_shape=jax.ShapeDtypeStruct((m, d), F32),
        compiler_params=_params("arbitrary"), name="ple",
    )(x, p, wg, wp, g, b)


def kernel(x_prompt, x_sample, p_prompt, p_sample, cache_k, cache_v, state_conv_b, state_conv_d, page_table, w_in, w_gate, b_gate, a_ln_g, a_ln_b, a_w_s, a_b_s, b_conv_w, d_conv_w, d_conv_b, d_ln_g, d_ln_b, w_branch, w_out, ln_g, ln_b, ffn_w_gate, ffn_w_up, ffn_w_down, moe_w_router, moe_b_router, moe_w_gate, moe_w_up, moe_w_down, ple_w_gate, ple_w_proj):
    depth = w_in.shape[0]
    alpha = (2 * depth) ** 0.25
    nbatch, seq, d = x_prompt.shape
    ns, dec_seq, _ = x_sample.shape
    assert dec_seq == 1, "the sample group decodes one token per sequence"
    n_pool, page = cache_k.shape[1], cache_k.shape[2]
    n_pages = page_table.shape[1]
    assert MOBA_BLOCK % page == 0 and (n_pages * page) % MOBA_BLOCK == 0 and (n_pages * page) % CHUNK == 0
    ppb = MOBA_BLOCK // page
    nbp = n_pages // ppb
    assert nbp >= MOBA_TOPK
    pt_flat = page_table.reshape(-1).astype(jnp.int32)
    row2 = lambda a: a.reshape(1, -1)

    xp = x_prompt
    xs = x_sample.reshape(ns, d)
    outs = {k: [] for k in ("kp", "vp", "ks", "vs", "cbp", "cbs", "cdp", "cds", "cvs")}
    for i in range(depth):
        w_in_i = w_in[i].astype(BF16)
        w_gate_i = w_gate[i].astype(BF16)
        w_branch_i = w_branch[i].astype(BF16)
        w_out_i = w_out[i].astype(BF16)
        alng, alnb, dlng, dlnb, dcb = (row2(a[i]) for a in (a_ln_g, a_ln_b, d_ln_g, d_ln_b, d_conv_b))
        bg = row2(b_gate[i])

        abias = jnp.repeat(a_b_s[i].T, HEAD_DIM, axis=1)
        ya, yb, yd, q, k, v, kb, vt, km, tb, td = _inmix_prompt(
            xp, w_in_i, alng, alnb, a_w_s[i], abias, b_conv_w[i], d_conv_w[i], dcb, dlng, dlnb)
        yc = _attn_prompt(q, kb, vt, km.reshape(nbatch, seq // MOBA_BLOCK, BW))
        outs["kp"].append(k.reshape(nbatch, seq, HEADS, HEAD_DIM))
        outs["vp"].append(v.reshape(nbatch, seq, HEADS, HEAD_DIM))
        outs["cbp"].append(tb)
        outs["cdp"].append(td)
        m = nbatch * seq
        x1p = _merge(xp.reshape(m, d), ya.reshape(m, BW), yb.reshape(m, BW), yc.reshape(m, BW), yd.reshape(m, BW),
                     w_gate_i, bg, w_branch_i, w_out_i, row2(ln_g[i, 0]), row2(ln_b[i, 0]), alpha=alpha)

        w00 = row2(jnp.repeat(a_w_s[i][:, 0, 0], HEAD_DIM))
        b0 = row2(jnp.repeat(a_b_s[i][:, 0], HEAD_DIM))
        hb = jnp.transpose(state_conv_b[i], (1, 0, 2))
        hd = jnp.transpose(state_conv_d[i], (1, 0, 2))
        sya, syb, syd, sq, sk, sv, svn, scbn, scdn = _inmix_sample(
            xs, w_in_i, alng, alnb, w00, b0, b_conv_w[i], hb, d_conv_w[i], dcb, hd, dlng, dlnb)
        ck = cache_k[i].reshape(n_pool, page, BW)
        cv = cache_v[i].reshape(n_pool, page, BW)
        sel = _select_sample(pt_flat, sq, _pagesum(ck), n_pages=n_pages, nbp=nbp, ppb=ppb)
        sel_flat = sel[:, :MOBA_TOPK, :HEADS].reshape(-1)
        syc = _attn_sample(pt_flat, sel_flat, ck, cv, sq, sk, sv, n_pages=n_pages, ppb=ppb)
        outs["ks"].append(sk.reshape(ns, 1, HEADS, HEAD_DIM))
        outs["vs"].append(sv.reshape(ns, 1, HEADS, HEAD_DIM))
        outs["cbs"].append(jnp.concatenate([state_conv_b[i][:, 1:], scbn[:, None]], axis=1))
        outs["cds"].append(jnp.concatenate([state_conv_d[i][:, 1:], scdn[:, None]], axis=1))
        outs["cvs"].append(svn.reshape(ns, 1, BW))
        x1s = _merge(xs, sya, syb, syc.reshape(ns, BW).astype(BF16), syd,
                     w_gate_i, bg, w_branch_i, w_out_i, row2(ln_g[i, 0]), row2(ln_b[i, 0]), alpha=alpha)

        g1, b1 = row2(ln_g[i, 1]), row2(ln_b[i, 1])
        j = i // 2
        if i % 2 == 0:
            wg, wu, wd = (w[j].astype(BF16) for w in (ffn_w_gate, ffn_w_up, ffn_w_down))
            x2p = _ffn(x1p, wg, wu, wd, g1, b1, alpha=alpha)
            x2s = _ffn(x1s, wg, wu, wd, g1, b1, alpha=alpha)
        else:
            wg, wu, wd = (w[j].astype(BF16) for w in (moe_w_gate, moe_w_up, moe_w_down))
            ne = moe_w_router.shape[2]
            wr = jnp.pad(moe_w_router[j], ((0, 0), (0, LANES - ne)))
            br = row2(jnp.pad(moe_b_router[j], (0, LANES - ne), constant_values=NEG))
            x2p = _moe(x1p, _router(x1p, wr, br), wg, wu, wd, g1, b1, alpha=alpha)
            x2s = _moe(x1s, _router(x1s, wr, br), wg, wu, wd, g1, b1, alpha=alpha)

        wpg = ple_w_gate[i].astype(BF16)
        wpp = ple_w_proj[i].astype(BF16)
        g2, b2 = row2(ln_g[i, 2]), row2(ln_b[i, 2])
        xp = _ple(x2p, p_prompt[i].reshape(m, -1), wpg, wpp, g2, b2, alpha=alpha).reshape(nbatch, seq, d)
        xs = _ple(x2s, p_sample[i].reshape(ns, -1), wpg, wpp, g2, b2, alpha=alpha)

    st = lambda key: jnp.stack(outs[key])
    return (xp, xs.reshape(ns, 1, d), st("kp"), st("vp"), st("ks"), st("vs"), st("cbp"), st("cbs"), st("cdp"),
            st("cds"), st("cvs"))
```

```python
import functools
import math

import jax
import jax.numpy as jnp
from jax import lax
from jax.experimental import pallas as pl
from jax.experimental.pallas import tpu as pltpu

F32 = jnp.float32
BF16 = jnp.bfloat16
HIGHEST = lax.Precision.HIGHEST

BW = 256
N_BRANCH = 4
CHUNK = 128
HEADS = 4
HEAD_DIM = BW // HEADS
HEAD_SHIFT = 6
MOBA_BLOCK = 256
MOBA_TOPK = 3
B_CONV = 3
D_CONV = 31
N_EXPERTS = 8
LN_EPS = 1e-5
NEG = -1e30
LOG2E = math.log2(math.e)
LANES = 128
SUBLANES = 8
V7X_VMEM_LIMIT = 56 * 1024 * 1024
CONV_ROWS = 64
B_HALO = 8
D_HALO = 32


def _params(*sem):
    return pltpu.CompilerParams(dimension_semantics=sem, vmem_limit_bytes=V7X_VMEM_LIMIT)


def _ln(x, g, b):
    mu = jnp.mean(x, axis=-1, keepdims=True)
    xc = x - mu
    var = jnp.mean(xc * xc, axis=-1, keepdims=True)
    return xc * lax.rsqrt(var + LN_EPS) * g + b


def _gelu(x):
    c = math.sqrt(2.0 / math.pi)
    return x * (0.5 * (1.0 + jnp.tanh(c * (x + 0.044715 * (x * x * x)))))


def _sigmoid(x):
    return 1.0 / (1.0 + jnp.exp(-x))


def _silu(x):
    return x * _sigmoid(x)


def _mm(a, b):
    if b.dtype == F32:
        return jnp.dot(a.astype(F32), b, precision=HIGHEST, preferred_element_type=F32)
    return jnp.dot(a.astype(BF16), b, preferred_element_type=F32)


def _mm_nt(a, b, **kw):
    return lax.dot_general(a, b, (((1,), (1,)), ((), ())), preferred_element_type=F32, **kw)


def _lane_head(shape):
    return lax.shift_right_logical(lax.broadcasted_iota(jnp.int32, shape, len(shape) - 1), HEAD_SHIFT)


def _row_tile(m, want):
    tm = min(want, m)
    assert m % tm == 0, (m, tm)
    return tm


def _inmix_prompt_kernel(x_ref, w_ref, alng_ref, alnb_ref, aws_ref, abias_ref, bcw_ref, dcw_ref, dcb_ref,
                         dlng_ref, dlnb_ref,
                         ya_ref, yb_ref, yd_ref, q_ref, k_ref, v_ref, kb_ref, vt_ref, km_ref, tb_ref, td_ref,
                         cb_buf, cd_buf, *, tm, scale):
    t = pl.program_id(1)
    last = pl.num_programs(1) - 1

    @pl.when(t == 0)
    def _():
        cb_buf[0:B_HALO, :] = jnp.zeros((B_HALO, BW), F32)
        cd_buf[0:D_HALO, :] = jnp.zeros((D_HALO, BW), F32)

    xb = x_ref[...].astype(BF16)

    def proj(j):
        return _mm(xb, w_ref[:, j * BW:(j + 1) * BW])

    gu = _gelu(proj(0))
    vnb = _ln(_gelu(proj(1)), alng_ref[...], alnb_ref[...]).astype(BF16)
    row = lax.broadcasted_iota(jnp.int32, (CHUNK, CHUNK), 0)
    col = lax.broadcasted_iota(jnp.int32, (CHUNK, CHUNK), 1)
    wts = [jnp.where(row >= col, aws_ref[h], 0.0).astype(BF16) for h in range(HEADS)]
    lane_head = _lane_head((CHUNK, BW))
    for c in range(tm // CHUNK):
        sl = slice(c * CHUNK, (c + 1) * CHUNK)
        vc = vnb[sl, :]
        s = abias_ref[...]
        for h in range(HEADS):
            s = s + _mm(wts[h], jnp.where(lane_head == h, vc, jnp.zeros_like(vc)))
        ya_ref[sl, :] = (gu[sl, :] * s).astype(BF16)

    bb = proj(2)
    cb_buf[B_HALO:B_HALO + tm, :] = proj(3) * proj(4)
    off_b = B_HALO - (B_CONV - 1)
    for r0 in range(0, tm, CONV_ROWS):
        acc = jnp.zeros((CONV_ROWS, BW), F32)
        for kk in range(B_CONV):
            acc = acc + bcw_ref[kk:kk + 1, :] * cb_buf[r0 + kk + off_b:r0 + kk + off_b + CONV_ROWS, :]
        yb_ref[r0:r0 + CONV_ROWS, :] = (bb[r0:r0 + CONV_ROWS, :] * acc).astype(BF16)

    @pl.when(t == last)
    def _():
        tb_ref[...] = cb_buf[tm + off_b:tm + B_HALO, :]

    cb_buf[0:B_HALO, :] = cb_buf[tm:tm + B_HALO, :]

    q_ref[...] = proj(5) * scale
    k = proj(6)
    v = proj(7)
    k_ref[...] = k
    v_ref[...] = v
    for i in range(tm // MOBA_BLOCK):
        sl = slice(i * MOBA_BLOCK, (i + 1) * MOBA_BLOCK)
        kblk = k[sl, :]
        kb_ref[i] = kblk.astype(BF16)
        vt_ref[i] = v[sl, :].T.astype(BF16)
        km_ref[i] = jnp.sum(kblk, axis=0, keepdims=True) * (1.0 / MOBA_BLOCK)

    cd_buf[D_HALO:D_HALO + tm, :] = proj(8) * _sigmoid(proj(9))
    off_d = D_HALO - (D_CONV - 1)
    for r0 in range(0, tm, CONV_ROWS):
        acc = jnp.zeros((CONV_ROWS, BW), F32) + dcb_ref[...]
        for kk in range(D_CONV):
            acc = acc + dcw_ref[kk:kk + 1, :] * cd_buf[r0 + kk + off_d:r0 + kk + off_d + CONV_ROWS, :]
        yd_ref[r0:r0 + CONV_ROWS, :] = _silu(_ln(acc, dlng_ref[...], dlnb_ref[...])).astype(BF16)

    @pl.when(t == last)
    def _():
        td_ref[...] = cd_buf[tm + off_d:tm + D_HALO, :]

    cd_buf[0:D_HALO, :] = cd_buf[tm:tm + D_HALO, :]


def _inmix_prompt(x, w_in, alng, alnb, aws, abias, bcw, dcw, dcb, dlng, dlnb):
    b, t, d = x.shape
    tm = _row_tile(t, 512)
    assert tm % MOBA_BLOCK == 0 and tm % CONV_ROWS == 0
    nbt = tm // MOBA_BLOCK
    nb = t // MOBA_BLOCK
    full = lambda shape: pl.BlockSpec(shape, lambda i, j: (0,) * len(shape))
    rows = lambda: pl.BlockSpec((None, tm, BW), lambda i, j: (i, j, 0))
    blocks = lambda: pl.BlockSpec((None, nbt, MOBA_BLOCK, MOBA_BLOCK), lambda i, j: (i, j, 0, 0))
    out_shape = (
        jax.ShapeDtypeStruct((b, t, BW), BF16),
        jax.ShapeDtypeStruct((b, t, BW), BF16),
        jax.ShapeDtypeStruct((b, t, BW), BF16),
        jax.ShapeDtypeStruct((b, t, BW), F32),
        jax.ShapeDtypeStruct((b, t, BW), F32),
        jax.ShapeDtypeStruct((b, t, BW), F32),
        jax.ShapeDtypeStruct((b, nb, MOBA_BLOCK, BW), BF16),
        jax.ShapeDtypeStruct((b, nb, BW, MOBA_BLOCK), BF16),
        jax.ShapeDtypeStruct((b, nb, 1, BW), F32),
        jax.ShapeDtypeStruct((b, B_CONV - 1, BW), F32),
        jax.ShapeDtypeStruct((b, D_CONV - 1, BW), F32),
    )
    out_specs = (
        rows(), rows(), rows(), rows(), rows(), rows(), blocks(), blocks(),
        pl.BlockSpec((None, nbt, 1, BW), lambda i, j: (i, j, 0, 0)),
        pl.BlockSpec((None, B_CONV - 1, BW), lambda i, j: (i, 0, 0)),
        pl.BlockSpec((None, D_CONV - 1, BW), lambda i, j: (i, 0, 0)),
    )
    in_specs = [
        pl.BlockSpec((None, tm, d), lambda i, j: (i, j, 0)),
        full(w_in.shape), full(alng.shape), full(alnb.shape), full(aws.shape), full(abias.shape),
        full(bcw.shape), full(dcw.shape), full(dcb.shape), full(dlng.shape), full(dlnb.shape),
    ]
    return pl.pallas_call(
        functools.partial(_inmix_prompt_kernel, tm=tm, scale=LOG2E / math.sqrt(HEAD_DIM)),
        grid=(b, t // tm), in_specs=in_specs, out_specs=out_specs, out_shape=out_shape,
        scratch_shapes=[pltpu.VMEM((B_HALO + tm, BW), F32), pltpu.VMEM((D_HALO + tm, BW), F32)],
        compiler_params=_params("arbitrary", "arbitrary"), name="inmix_prompt",
    )(x, w_in, alng, alnb, aws, abias, bcw, dcw, dcb, dlng, dlnb)


def _inmix_sample_kernel(x_ref, w_ref, alng_ref, alnb_ref, w00_ref, b0_ref, bcw_ref, hb_ref, dcw_ref, dcb_ref,
                         hd_ref, dlng_ref, dlnb_ref,
                         ya_ref, yb_ref, yd_ref, q_ref, k_ref, v_ref, vn_ref, cbn_ref, cdn_ref, *, scale):
    x = x_ref[...]

    def proj(j):
        return _mm(x, w_ref[:, j * BW:(j + 1) * BW])

    vn = _ln(_gelu(proj(1)), alng_ref[...], alnb_ref[...])
    vn_ref[...] = vn
    ya_ref[...] = _gelu(proj(0)) * (vn * w00_ref[...] + b0_ref[...])

    cbn = proj(3) * proj(4)
    cbn_ref[...] = cbn
    conv = bcw_ref[B_CONV - 1:B_CONV, :] * cbn
    for kk in range(B_CONV - 1):
        conv = conv + bcw_ref[kk:kk + 1, :] * hb_ref[kk]
    yb_ref[...] = proj(2) * conv

    q_ref[...] = proj(5) * scale
    k_ref[...] = proj(6)
    v_ref[...] = proj(7)

    cdn = proj(8) * _sigmoid(proj(9))
    cdn_ref[...] = cdn
    conv = dcb_ref[...] + dcw_ref[D_CONV - 1:D_CONV, :] * cdn
    for kk in range(D_CONV - 1):
        conv = conv + dcw_ref[kk:kk + 1, :] * hd_ref[kk]
    yd_ref[...] = _silu(_ln(conv, dlng_ref[...], dlnb_ref[...]))


def _inmix_sample(x, w_in, alng, alnb, w00, b0, bcw, hb, dcw, dcb, hd, dlng, dlnb):
    s = x.shape[0]
    args = (x, w_in, alng, alnb, w00, b0, bcw, hb, dcw, dcb, hd, dlng, dlnb)
    full = lambda shape: pl.BlockSpec(shape, lambda i: (0,) * len(shape))
    row = lambda dt: jax.ShapeDtypeStruct((s, BW), dt)
    out_shape = tuple(row(F32) for _ in range(9))
    return pl.pallas_call(
        functools.partial(_inmix_sample_kernel, scale=1.0 / math.sqrt(HEAD_DIM)),
        grid=(1,), in_specs=[full(a.shape) for a in args], out_specs=tuple(full((s, BW)) for _ in out_shape),
        out_shape=out_shape, compiler_params=_params("arbitrary"), name="inmix_sample",
    )(*args)


def _attn_prompt_kernel(q_ref, kb_ref, vt_ref, km_ref, yc_ref, bias_scr, qh_scr, m_scr, l_scr, acc_scr, s_a, s_b,
                        *, nb):
    c = pl.program_id(1)
    qb = MOBA_BLOCK
    q = q_ref[...]
    q16 = q.astype(BF16)
    km = km_ref[...]
    lane_head_q = _lane_head((qb, BW))
    lane_head_km = _lane_head((nb, BW))
    blk = lax.broadcasted_iota(jnp.int32, (nb, qb), 0)
    key_pos = lax.broadcasted_iota(jnp.int32, (qb, qb), 0)
    q_pos = lax.broadcasted_iota(jnp.int32, (qb, qb), 1)
    valid = blk < c

    for h in range(HEADS):
        gate = _mm_nt(jnp.where(lane_head_km == h, km, 0.0), q, precision=HIGHEST)
        g = jnp.where(valid, gate, -jnp.inf)
        sel = jnp.zeros((nb, qb), F32)
        for _ in range(min(MOBA_TOPK, nb)):
            m = jnp.max(g, axis=0, keepdims=True)
            idx = jnp.min(jnp.where(g == m, blk, nb), axis=0, keepdims=True)
            hit = blk == idx
            sel = jnp.where(hit, 1.0, sel)
            g = jnp.where(hit, -jnp.inf, g)
        bias_scr[h] = jnp.where(jnp.logical_and(sel > 0.5, valid), 0.0, NEG)

        qh = jnp.where(lane_head_q == h, q16, jnp.zeros_like(q16))
        qh_scr[h * qb:(h + 1) * qb, :] = qh
        hs = slice(h * HEAD_DIM, (h + 1) * HEAD_DIM)
        s = jnp.where(key_pos <= q_pos, _mm_nt(kb_ref[c], qh), NEG)
        m = jnp.max(s, axis=0, keepdims=True)
        p = jnp.exp2(s - m)
        m_scr[h] = m
        l_scr[h] = jnp.sum(p, axis=0, keepdims=True)
        acc_scr[hs, :] = _mm(vt_ref[c, hs, :], p.astype(BF16))

    s_a[...] = _mm_nt(kb_ref[0], qh_scr[...])

    def step(j, src, dst):
        dst[...] = _mm_nt(kb_ref[jnp.minimum(j + 1, c - 1)], qh_scr[...])
        for h in range(HEADS):
            hs = slice(h * HEAD_DIM, (h + 1) * HEAD_DIM)
            s = src[:, h * qb:(h + 1) * qb] + bias_scr[h, pl.ds(j, 1), :]
            m = m_scr[h]
            m_new = jnp.maximum(m, jnp.max(s, axis=0, keepdims=True))
            a = jnp.exp2(m - m_new)
            p = jnp.exp2(s - m_new)
            m_scr[h] = m_new
            l_scr[h] = a * l_scr[h] + jnp.sum(p, axis=0, keepdims=True)
            acc_scr[hs, :] = a * acc_scr[hs, :] + _mm(vt_ref[j, hs, :], p.astype(BF16))

    def pair(i, carry):
        step(2 * i, s_a, s_b)
        step(2 * i + 1, s_b, s_a)
        return carry

    lax.fori_loop(0, lax.shift_right_logical(c, 1), pair, 0)

    @pl.when(lax.rem(c, 2) == 1)
    def _():
        step(c - 1, s_a, s_b)

    for h in range(HEADS):
        hs = slice(h * HEAD_DIM, (h + 1) * HEAD_DIM)
        acc_scr[hs, :] = acc_scr[hs, :] / l_scr[h]
    yc_ref[...] = acc_scr[...].T.astype(BF16)


def _attn_prompt(q, kb, vt, km):
    b, t, _ = q.shape
    nb = t // MOBA_BLOCK
    return pl.pallas_call(
        functools.partial(_attn_prompt_kernel, nb=nb),
        grid=(b, nb),
        in_specs=[
            pl.BlockSpec((None, MOBA_BLOCK, BW), lambda i, j: (i, j, 0)),
            pl.BlockSpec((None, nb, MOBA_BLOCK, BW), lambda i, j: (i, 0, 0, 0)),
            pl.BlockSpec((None, nb, BW, MOBA_BLOCK), lambda i, j: (i, 0, 0, 0)),
            pl.BlockSpec((None, nb, BW), lambda i, j: (i, 0, 0)),
        ],
        out_specs=pl.BlockSpec((None, MOBA_BLOCK, BW), lambda i, j: (i, j, 0)),
        out_shape=jax.ShapeDtypeStruct((b, t, BW), BF16),
        scratch_shapes=[pltpu.VMEM((HEADS, nb, MOBA_BLOCK), F32), pltpu.VMEM((HEADS * MOBA_BLOCK, BW), BF16),
                        pltpu.VMEM((HEADS, 1, MOBA_BLOCK), F32), pltpu.VMEM((HEADS, 1, MOBA_BLOCK), F32),
                        pltpu.VMEM((BW, MOBA_BLOCK), F32), pltpu.VMEM((MOBA_BLOCK, HEADS * MOBA_BLOCK), F32),
                        pltpu.VMEM((MOBA_BLOCK, HEADS * MOBA_BLOCK), F32)],
        compiler_params=_params("arbitrary", "arbitrary"), name="attn_prompt",
    )(q, kb, vt, km)


def _attn_sample_kernel(pt_ref, ck_ref, cv_ref, q_ref, kn_ref, vn_ref, o_ref, kbuf, vbuf, bsum, ksem, vsem,
                        *, layer, n_pages, ppb):
    b = pl.program_id(0)
    ns = pl.num_programs(0)
    slot = lax.rem(b, 2)
    nbp = n_pages // ppb
    per_head = MOBA_TOPK * ppb

    def k_copy(sample, i, sl):
        return pltpu.make_async_copy(ck_ref.at[layer, pt_ref[sample * n_pages + i]], kbuf.at[sl, i], ksem.at[sl])

    @pl.when(b == 0)
    def _():
        for i in range(n_pages):
            k_copy(0, i, 0).start()

    for i in range(n_pages):
        k_copy(b, i, slot).wait()

    @pl.when(b + 1 < ns)
    def _():
        for i in range(n_pages):
            k_copy(b + 1, i, 1 - slot).start()

    q = q_ref[...]
    group = SUBLANES

    def block_sums(jg, carry):
        rows = [[] for _ in range(HEADS)]
        for jj in range(group):
            tot = None
            for u in range(ppb):
                pr = kbuf[slot, (jg * group + jj) * ppb + u] * q
                tot = pr if tot is None else tot + pr
            for h in range(HEADS):
                rows[h].append(jnp.sum(tot[h * HEAD_DIM:(h + 1) * HEAD_DIM, :], axis=0, keepdims=True))
        for h in range(HEADS):
            bsum[h, pl.ds(pl.multiple_of(jg * group, group), group), :] = jnp.concatenate(rows[h], axis=0)
        return carry

    lax.fori_loop(0, nbp // group, block_sums, 0)

    blk = lax.broadcasted_iota(jnp.int32, (nbp, 1), 0)
    sel = []
    for h in range(HEADS):
        g = jnp.sum(bsum[h], axis=1, keepdims=True) * (1.0 / MOBA_BLOCK)
        for _ in range(MOBA_TOPK):
            idx = jnp.min(jnp.where(g == jnp.max(g), blk, nbp))
            sel.append(idx)
            g = jnp.where(blk == idx, -jnp.inf, g)

    def v_copy(h, r, u):
        n = h * per_head + r * ppb + u
        page = pt_ref[b * n_pages + sel[h * MOBA_TOPK + r] * ppb + u]
        return pltpu.make_async_copy(cv_ref.at[layer, page, pl.ds(h * HEAD_DIM, HEAD_DIM)], vbuf.at[n], vsem.at[0])

    v_copies = [v_copy(h, r, u) for h in range(HEADS) for r in range(MOBA_TOPK) for u in range(ppb)]
    for cp in v_copies:
        cp.start()

    kn = kn_ref[...]
    vn = vn_ref[...]
    probs = []
    for h in range(HEADS):
        hs = slice(h * HEAD_DIM, (h + 1) * HEAD_DIM)
        qh = q[hs, :]
        s_new = jnp.sum(qh * kn[hs, :], axis=0, keepdims=True)
        ss = []
        for r in range(MOBA_TOPK):
            for u in range(ppb):
                kp = kbuf[slot, sel[h * MOBA_TOPK + r] * ppb + u, hs, :]
                ss.append(jnp.sum(kp * qh, axis=0, keepdims=True))
        m = s_new
        for s in ss:
            m = jnp.maximum(m, jnp.max(s, axis=1, keepdims=True))
        p_new = jnp.exp(s_new - m)
        ps = [jnp.exp(s - m) for s in ss]
        l = p_new
        for p in ps:
            l = l + jnp.sum(p, axis=1, keepdims=True)
        probs.append((p_new, ps, l))

    for cp in v_copies:
        cp.wait()

    for h in range(HEADS):
        hs = slice(h * HEAD_DIM, (h + 1) * HEAD_DIM)
        p_new, ps, l = probs[h]
        o = p_new * vn[hs, :]
        for n, p in enumerate(ps):
            o = o + jnp.sum(vbuf[h * per_head + n] * p, axis=1, keepdims=True)
        o_ref[hs, :] = o / l


def _attn_sample(pt_flat, cache_k, cache_v, q, kn, vn, *, layer, n_pages, ppb):
    s = q.shape[0]
    page = cache_k.shape[3]
    n_sel = HEADS * MOBA_TOPK * ppb
    nbp = n_pages // ppb
    assert nbp % SUBLANES == 0
    col = lambda: pl.BlockSpec((None, BW, 1), lambda i, pt: (i, 0, 0))
    grid_spec = pltpu.PrefetchScalarGridSpec(
        num_scalar_prefetch=1, grid=(s,),
        in_specs=[pl.BlockSpec(memory_space=pl.ANY), pl.BlockSpec(memory_space=pl.ANY), col(), col(), col()],
        out_specs=col(),
        scratch_shapes=[pltpu.VMEM((2, n_pages, BW, page), F32), pltpu.VMEM((n_sel, HEAD_DIM, page), F32),
                        pltpu.VMEM((HEADS, nbp, page), F32),
                        pltpu.SemaphoreType.DMA((2,)), pltpu.SemaphoreType.DMA((1,))],
    )
    return pl.pallas_call(
        functools.partial(_attn_sample_kernel, layer=layer, n_pages=n_pages, ppb=ppb),
        grid_spec=grid_spec, out_shape=jax.ShapeDtypeStruct((s, BW, 1), F32),
        compiler_params=_params("arbitrary"), name="attn_sample",
    )(pt_flat, cache_k, cache_v, q[:, :, None], kn[:, :, None], vn[:, :, None])


def _merge_kernel(x_ref, ya_ref, yb_ref, yc_ref, yd_ref, wg_ref, bg_ref, wbr_ref, wo_ref, g_ref, b_ref, o_ref,
                  *, alpha):
    x = x_ref[...]
    xw = x.astype(wg_ref.dtype)
    d = x.shape[1]
    acc = None
    for j, y_ref in enumerate((ya_ref, yb_ref, yc_ref, yd_ref)):
        gate = _sigmoid(_mm(xw, wg_ref[:, j * d:(j + 1) * d]) + bg_ref[:, j * d:(j + 1) * d])
        term = gate * _mm(y_ref[...], wbr_ref[j])
        acc = term if acc is None else acc + term
    out = _mm(acc, wo_ref[...])
    o_ref[...] = _ln(alpha * x + out, g_ref[...], b_ref[...])


def _merge(x, ya, yb, yc, yd, wg, bg, wbr, wo, g, b, *, alpha):
    m, d = x.shape
    tm = _row_tile(m, 256)
    full = lambda a: pl.BlockSpec(a.shape, lambda i: (0,) * a.ndim, pipeline_mode=pl.Buffered(1))
    rows = lambda w: pl.BlockSpec((tm, w), lambda i: (i, 0))
    return pl.pallas_call(
        functools.partial(_merge_kernel, alpha=alpha),
        grid=(m // tm,),
        in_specs=[rows(d), rows(BW), rows(BW), rows(BW), rows(BW), full(wg), full(bg), full(wbr), full(wo),
                  full(g), full(b)],
        out_specs=rows(d), out_shape=jax.ShapeDtypeStruct((m, d), F32),
        compiler_params=_params("arbitrary"), name="merge",
    )(x, ya, yb, yc, yd, wg, bg, wbr, wo, g, b)


def _ffn_kernel(x_ref, wg_ref, wu_ref, wd_ref, g_ref, b_ref, o_ref, acc_ref, *, alpha):
    f = pl.program_id(1)

    @pl.when(f == 0)
    def _():
        acc_ref[...] = jnp.zeros_like(acc_ref)

    xw = x_ref[...].astype(wg_ref.dtype)
    h = _silu(_mm(xw, wg_ref[...])) * _mm(xw, wu_ref[...])
    acc_ref[...] += _mm(h, wd_ref[...])

    @pl.when(f == pl.num_programs(1) - 1)
    def _():
        o_ref[...] = _ln(alpha * x_ref[...] + acc_ref[...], g_ref[...], b_ref[...])


def _ffn_tile(dff):
    return next(c for c in (1408, 1024, 896, 768, 512, 384, 256, 128) if dff % c == 0)


def _ffn(x, wg, wu, wd, g, b, *, alpha):
    m, d = x.shape
    dff = wg.shape[1]
    tm = _row_tile(m, 512)
    tf = _ffn_tile(dff)
    full = lambda a: pl.BlockSpec(a.shape, lambda i, f: (0,) * a.ndim)
    return pl.pallas_call(
        functools.partial(_ffn_kernel, alpha=alpha),
        grid=(m // tm, dff // tf),
        in_specs=[pl.BlockSpec((tm, d), lambda i, f: (i, 0)), pl.BlockSpec((d, tf), lambda i, f: (0, f)),
                  pl.BlockSpec((d, tf), lambda i, f: (0, f)), pl.BlockSpec((tf, d), lambda i, f: (f, 0)),
                  full(g), full(b)],
        out_specs=pl.BlockSpec((tm, d), lambda i, f: (i, 0)), out_shape=jax.ShapeDtypeStruct((m, d), F32),
        scratch_shapes=[pltpu.VMEM((tm, d), F32)],
        compiler_params=_params("arbitrary", "arbitrary"), name="ffn",
    )(x, wg, wu, wd, g, b)


def _router_kernel(x_ref, wr_ref, br_ref, o_ref):
    lg = jnp.dot(x_ref[...], wr_ref[...], precision=HIGHEST, preferred_element_type=F32) + br_ref[...]
    lane = lax.broadcasted_iota(jnp.int32, lg.shape, 1)
    m1 = jnp.max(lg, axis=1, keepdims=True)
    hit1 = lane == jnp.min(jnp.where(lg == m1, lane, LANES), axis=1, keepdims=True)
    lg2 = jnp.where(hit1, -jnp.inf, lg)
    m2 = jnp.max(lg2, axis=1, keepdims=True)
    hit2 = lane == jnp.min(jnp.where(lg2 == m2, lane, LANES), axis=1, keepdims=True)
    e = jnp.exp(m2 - m1)
    o_ref[...] = jnp.where(hit1, 1.0 / (1.0 + e), 0.0) + jnp.where(hit2, e / (1.0 + e), 0.0)


def _router(x, wr, br):
    m, d = x.shape
    tm = _row_tile(m, 512)
    full = lambda a: pl.BlockSpec(a.shape, lambda i: (0,) * a.ndim)
    return pl.pallas_call(
        _router_kernel, grid=(m // tm,),
        in_specs=[pl.BlockSpec((tm, d), lambda i: (i, 0)), full(wr), full(br)],
        out_specs=pl.BlockSpec((tm, LANES), lambda i: (i, 0)), out_shape=jax.ShapeDtypeStruct((m, LANES), F32),
        compiler_params=_params("arbitrary"), name="router",
    )(x, wr, br)


def _moe_kernel(x_ref, c_ref, wg_ref, wu_ref, wd_ref, g_ref, b_ref, o_ref, acc_ref, *, alpha):
    e = pl.program_id(1)
    f = pl.program_id(2)

    @pl.when(jnp.logical_and(e == 0, f == 0))
    def _():
        acc_ref[...] = jnp.zeros_like(acc_ref)

    comb = c_ref[...]
    lane = lax.broadcasted_iota(jnp.int32, comb.shape, 1)
    ce = jnp.sum(jnp.where(lane == e, comb, 0.0), axis=1, keepdims=True)
    xw = x_ref[...].astype(wg_ref.dtype)
    h = _silu(_mm(xw, wg_ref[...])) * _mm(xw, wu_ref[...])
    acc_ref[...] += ce * _mm(h, wd_ref[...])

    @pl.when(jnp.logical_and(e == pl.num_programs(1) - 1, f == pl.num_programs(2) - 1))
    def _():
        o_ref[...] = _ln(alpha * x_ref[...] + acc_ref[...], g_ref[...], b_ref[...])


def _moe(x, comb, wg, wu, wd, g, b, *, alpha):
    m, d = x.shape
    ne, _, dff = wg.shape
    tm = _row_tile(m, 512)
    tf = _ffn_tile(dff)
    full = lambda a: pl.BlockSpec(a.shape, lambda i, e, f: (0,) * a.ndim)
    return pl.pallas_call(
        functools.partial(_moe_kernel, alpha=alpha),
        grid=(m // tm, ne, dff // tf),
        in_specs=[pl.BlockSpec((tm, d), lambda i, e, f: (i, 0)), pl.BlockSpec((tm, LANES), lambda i, e, f: (i, 0)),
                  pl.BlockSpec((None, d, tf), lambda i, e, f: (e, 0, f)),
                  pl.BlockSpec((None, d, tf), lambda i, e, f: (e, 0, f)),
                  pl.BlockSpec((None, tf, d), lambda i, e, f: (e, f, 0)), full(g), full(b)],
        out_specs=pl.BlockSpec((tm, d), lambda i, e, f: (i, 0)), out_shape=jax.ShapeDtypeStruct((m, d), F32),
        scratch_shapes=[pltpu.VMEM((tm, d), F32)],
        compiler_params=_params("arbitrary", "arbitrary", "arbitrary"), name="moe",
    )(x, comb, wg, wu, wd, g, b)


def _ple_kernel(x_ref, p_ref, wg_ref, wp_ref, g_ref, b_ref, o_ref, *, alpha):
    x = x_ref[...]
    gate = _sigmoid(_mm(x, wg_ref[...]))
    o_ref[...] = _ln(alpha * x + gate * _mm(p_ref[...], wp_ref[...]), g_ref[...], b_ref[...])


def _ple(x, p, wg, wp, g, b, *, alpha):
    m, d = x.shape
    tm = _row_tile(m, 512)
    full = lambda a: pl.BlockSpec(a.shape, lambda i: (0,) * a.ndim)
    return pl.pallas_call(
        functools.partial(_ple_kernel, alpha=alpha),
        grid=(m // tm,),
        in_specs=[pl.BlockSpec((tm, d), lambda i: (i, 0)), pl.BlockSpec((tm, p.shape[1]), lambda i: (i, 0)),
                  full(wg), full(wp), full(g), full(b)],
        out_specs=pl.BlockSpec((tm, d), lambda i: (i, 0)), out_shape=jax.ShapeDtypeStruct((m, d), F32),
        compiler_params=_params("arbitrary"), name="ple",
    )(x, p, wg, wp, g, b)


def kernel(x_prompt, x_sample, p_prompt, p_sample, cache_k, cache_v, state_conv_b, state_conv_d, page_table, w_in, w_gate, b_gate, a_ln_g, a_ln_b, a_w_s, a_b_s, b_conv_w, d_conv_w, d_conv_b, d_ln_g, d_ln_b, w_branch, w_out, ln_g, ln_b, ffn_w_gate, ffn_w_up, ffn_w_down, moe_w_router, moe_b_router, moe_w_gate, moe_w_up, moe_w_down, ple_w_gate, ple_w_proj):
    depth = w_in.shape[0]
    alpha = (2 * depth) ** 0.25
    nbatch, seq, d = x_prompt.shape
    ns, dec_seq, _ = x_sample.shape
    assert dec_seq == 1, "the sample group decodes one token per sequence"
    n_pool, page = cache_k.shape[1], cache_k.shape[2]
    n_pages = page_table.shape[1]
    assert MOBA_BLOCK % page == 0 and (n_pages * page) % MOBA_BLOCK == 0 and (n_pages * page) % CHUNK == 0
    ppb = MOBA_BLOCK // page
    nbp = n_pages // ppb
    assert nbp >= MOBA_TOPK
    pt_flat = page_table.reshape(-1).astype(jnp.int32)
    ck_all = jnp.transpose(cache_k, (0, 1, 3, 4, 2)).reshape(depth, n_pool, BW, page)
    cv_all = jnp.transpose(cache_v, (0, 1, 3, 4, 2)).reshape(depth, n_pool, BW, page)
    row2 = lambda a: a.reshape(1, -1)

    xp = x_prompt
    xs = x_sample.reshape(ns, d)
    outs = {k: [] for k in ("kp", "vp", "ks", "vs", "cbp", "cbs", "cdp", "cds", "cvs")}
    for i in range(depth):
        w_in_i = w_in[i].astype(BF16)
        w_gate_i = w_gate[i].astype(BF16)
        w_branch_i = w_branch[i].astype(BF16)
        w_out_i = w_out[i].astype(BF16)
        alng, alnb, dlng, dlnb, dcb = (row2(a[i]) for a in (a_ln_g, a_ln_b, d_ln_g, d_ln_b, d_conv_b))
        bg = row2(b_gate[i])

        abias = jnp.repeat(a_b_s[i].T, HEAD_DIM, axis=1)
        ya, yb, yd, q, k, v, kb, vt, km, tb, td = _inmix_prompt(
            xp, w_in_i, alng, alnb, a_w_s[i], abias, b_conv_w[i], d_conv_w[i], dcb, dlng, dlnb)
        yc = _attn_prompt(q, kb, vt, km.reshape(nbatch, seq // MOBA_BLOCK, BW))
        outs["kp"].append(k.reshape(nbatch, seq, HEADS, HEAD_DIM))
        outs["vp"].append(v.reshape(nbatch, seq, HEADS, HEAD_DIM))
        outs["cbp"].append(tb)
        outs["cdp"].append(td)
        m = nbatch * seq
        x1p = _merge(xp.reshape(m, d), ya.reshape(m, BW), yb.reshape(m, BW), yc.reshape(m, BW), yd.reshape(m, BW),
                     w_gate_i, bg, w_branch_i, w_out_i, row2(ln_g[i, 0]), row2(ln_b[i, 0]), alpha=alpha)

        w00 = row2(jnp.repeat(a_w_s[i][:, 0, 0], HEAD_DIM))
        b0 = row2(jnp.repeat(a_b_s[i][:, 0], HEAD_DIM))
        hb = jnp.transpose(state_conv_b[i], (1, 0, 2))
        hd = jnp.transpose(state_conv_d[i], (1, 0, 2))
        sya, syb, syd, sq, sk, sv, svn, scbn, scdn = _inmix_sample(
            xs, w_in[i], alng, alnb, w00, b0, b_conv_w[i], hb, d_conv_w[i], dcb, hd, dlng, dlnb)
        syc = _attn_sample(pt_flat, ck_all, cv_all, sq, sk, sv, layer=i, n_pages=n_pages, ppb=ppb)
        outs["ks"].append(sk.reshape(ns, 1, HEADS, HEAD_DIM))
        outs["vs"].append(sv.reshape(ns, 1, HEADS, HEAD_DIM))
        outs["cbs"].append(jnp.concatenate([state_conv_b[i][:, 1:], scbn[:, None]], axis=1))
        outs["cds"].append(jnp.concatenate([state_conv_d[i][:, 1:], scdn[:, None]], axis=1))
        outs["cvs"].append(svn.reshape(ns, 1, BW))
        x1s = _merge(xs, sya, syb, syc.reshape(ns, BW), syd,
                     w_gate[i], bg, w_branch[i], w_out[i], row2(ln_g[i, 0]), row2(ln_b[i, 0]), alpha=alpha)

        g1, b1 = row2(ln_g[i, 1]), row2(ln_b[i, 1])
        j = i // 2
        if i % 2 == 0:
            wg, wu, wd = (w[j].astype(BF16) for w in (ffn_w_gate, ffn_w_up, ffn_w_down))
            x2p = _ffn(x1p, wg, wu, wd, g1, b1, alpha=alpha)
            x2s = _ffn(x1s, ffn_w_gate[j], ffn_w_up[j], ffn_w_down[j], g1, b1, alpha=alpha)
        else:
            wg, wu, wd = (w[j].astype(BF16) for w in (moe_w_gate, moe_w_up, moe_w_down))
            ne = moe_w_router.shape[2]
            wr = jnp.pad(moe_w_router[j], ((0, 0), (0, LANES - ne)))
            br = row2(jnp.pad(moe_b_router[j], (0, LANES - ne), constant_values=NEG))
            x2p = _moe(x1p, _router(x1p, wr, br), wg, wu, wd, g1, b1, alpha=alpha)
            x2s = _moe(x1s, _router(x1s, wr, br), wg, wu, wd, g1, b1, alpha=alpha)

        wpg = ple_w_gate[i].astype(BF16)
        wpp = ple_w_proj[i].astype(BF16)
        g2, b2 = row2(ln_g[i, 2]), row2(ln_b[i, 2])
        xp = _ple(x2p, p_prompt[i].reshape(m, -1), wpg, wpp, g2, b2, alpha=alpha).reshape(nbatch, seq, d)
        xs = _ple(x2s, p_sample[i].reshape(ns, -1), ple_w_gate[i], ple_w_proj[i], g2, b2, alpha=alpha)

    st = lambda key: jnp.stack(outs[key])
    return (xp, xs.reshape(ns, 1, d), st("kp"), st("vp"), st("ks"), st("vs"), st("cbp"), st("cbs"), st("cdp"),
            st("cds"), st("cvs"))
```

```python
import functools
import math

import jax
import jax.numpy as jnp
from jax import lax
from jax.experimental import pallas as pl
from jax.experimental.pallas import tpu as pltpu

F32 = jnp.float32
BF16 = jnp.bfloat16
HIGHEST = lax.Precision.HIGHEST

BW = 256
N_BRANCH = 4
CHUNK = 128
HEADS = 4
HEAD_DIM = BW // HEADS
HEAD_SHIFT = 6
MOBA_BLOCK = 256
MOBA_TOPK = 3
B_CONV = 3
D_CONV = 31
N_EXPERTS = 8
LN_EPS = 1e-5
NEG = -1e30
LOG2E = math.log2(math.e)
LANES = 128
SUBLANES = 8
V7X_VMEM_LIMIT = 56 * 1024 * 1024
CONV_ROWS = 64
B_HALO = 8
D_HALO = 32


def _params(*sem):
    return pltpu.CompilerParams(dimension_semantics=sem, vmem_limit_bytes=V7X_VMEM_LIMIT)


def _ln(x, g, b):
    mu = jnp.mean(x, axis=-1, keepdims=True)
    xc = x - mu
    var = jnp.mean(xc * xc, axis=-1, keepdims=True)
    return xc * lax.rsqrt(var + LN_EPS) * g + b


def _gelu(x):
    c = math.sqrt(2.0 / math.pi)
    return x * (0.5 * (1.0 + jnp.tanh(c * (x + 0.044715 * (x * x * x)))))


def _sigmoid(x):
    return 1.0 / (1.0 + jnp.exp(-x))


def _silu(x):
    return x * _sigmoid(x)


def _mm(a, b):
    if b.dtype == F32:
        return jnp.dot(a.astype(F32), b, precision=HIGHEST, preferred_element_type=F32)
    return jnp.dot(a.astype(BF16), b, preferred_element_type=F32)


def _mm_nt(a, b, **kw):
    return lax.dot_general(a, b, (((1,), (1,)), ((), ())), preferred_element_type=F32, **kw)


def _lane_head(shape):
    return lax.shift_right_logical(lax.broadcasted_iota(jnp.int32, shape, len(shape) - 1), HEAD_SHIFT)


def _row_tile(m, want):
    tm = min(want, m)
    assert m % tm == 0, (m, tm)
    return tm


def _inmix_prompt_kernel(x_ref, w_ref, alng_ref, alnb_ref, aws_ref, abias_ref, bcw_ref, dcw_ref, dcb_ref,
                         dlng_ref, dlnb_ref,
                         ya_ref, yb_ref, yd_ref, q_ref, k_ref, v_ref, kb_ref, vt_ref, km_ref, tb_ref, td_ref,
                         cb_buf, cd_buf, *, tm, scale):
    t = pl.program_id(1)
    last = pl.num_programs(1) - 1

    @pl.when(t == 0)
    def _():
        cb_buf[0:B_HALO, :] = jnp.zeros((B_HALO, BW), F32)
        cd_buf[0:D_HALO, :] = jnp.zeros((D_HALO, BW), F32)

    xb = x_ref[...].astype(BF16)

    def proj(j):
        return _mm(xb, w_ref[:, j * BW:(j + 1) * BW])

    gu = _gelu(proj(0))
    vnb = _ln(_gelu(proj(1)), alng_ref[...], alnb_ref[...]).astype(BF16)
    row = lax.broadcasted_iota(jnp.int32, (CHUNK, CHUNK), 0)
    col = lax.broadcasted_iota(jnp.int32, (CHUNK, CHUNK), 1)
    wts = [jnp.where(row >= col, aws_ref[h], 0.0).astype(BF16) for h in range(HEADS)]
    lane_head = _lane_head((CHUNK, BW))
    for c in range(tm // CHUNK):
        sl = slice(c * CHUNK, (c + 1) * CHUNK)
        vc = vnb[sl, :]
        s = abias_ref[...]
        for h in range(HEADS):
            s = s + _mm(wts[h], jnp.where(lane_head == h, vc, jnp.zeros_like(vc)))
        ya_ref[sl, :] = (gu[sl, :] * s).astype(BF16)

    bb = proj(2)
    cb_buf[B_HALO:B_HALO + tm, :] = proj(3) * proj(4)
    off_b = B_HALO - (B_CONV - 1)
    for r0 in range(0, tm, CONV_ROWS):
        acc = jnp.zeros((CONV_ROWS, BW), F32)
        for kk in range(B_CONV):
            acc = acc + bcw_ref[kk:kk + 1, :] * cb_buf[r0 + kk + off_b:r0 + kk + off_b + CONV_ROWS, :]
        yb_ref[r0:r0 + CONV_ROWS, :] = (bb[r0:r0 + CONV_ROWS, :] * acc).astype(BF16)

    @pl.when(t == last)
    def _():
        tb_ref[...] = cb_buf[tm + off_b:tm + B_HALO, :]

    cb_buf[0:B_HALO, :] = cb_buf[tm:tm + B_HALO, :]

    q_ref[...] = proj(5) * scale
    k = proj(6)
    v = proj(7)
    k_ref[...] = k
    v_ref[...] = v
    for i in range(tm // MOBA_BLOCK):
        sl = slice(i * MOBA_BLOCK, (i + 1) * MOBA_BLOCK)
        kblk = k[sl, :]
        kb_ref[i] = kblk.astype(BF16)
        vt_ref[i] = v[sl, :].T.astype(BF16)
        km_ref[i] = jnp.sum(kblk, axis=0, keepdims=True) * (1.0 / MOBA_BLOCK)

    cd_buf[D_HALO:D_HALO + tm, :] = proj(8) * _sigmoid(proj(9))
    off_d = D_HALO - (D_CONV - 1)
    for r0 in range(0, tm, CONV_ROWS):
        acc = jnp.zeros((CONV_ROWS, BW), F32) + dcb_ref[...]
        for kk in range(D_CONV):
            acc = acc + dcw_ref[kk:kk + 1, :] * cd_buf[r0 + kk + off_d:r0 + kk + off_d + CONV_ROWS, :]
        yd_ref[r0:r0 + CONV_ROWS, :] = _silu(_ln(acc, dlng_ref[...], dlnb_ref[...])).astype(BF16)

    @pl.when(t == last)
    def _():
        td_ref[...] = cd_buf[tm + off_d:tm + D_HALO, :]

    cd_buf[0:D_HALO, :] = cd_buf[tm:tm + D_HALO, :]


def _inmix_prompt(x, w_in, alng, alnb, aws, abias, bcw, dcw, dcb, dlng, dlnb):
    b, t, d = x.shape
    tm = _row_tile(t, 512)
    assert tm % MOBA_BLOCK == 0 and tm % CONV_ROWS == 0
    nbt = tm // MOBA_BLOCK
    nb = t // MOBA_BLOCK
    full = lambda shape: pl.BlockSpec(shape, lambda i, j: (0,) * len(shape))
    rows = lambda: pl.BlockSpec((None, tm, BW), lambda i, j: (i, j, 0))
    blocks = lambda: pl.BlockSpec((None, nbt, MOBA_BLOCK, MOBA_BLOCK), lambda i, j: (i, j, 0, 0))
    out_shape = (
        jax.ShapeDtypeStruct((b, t, BW), BF16),
        jax.ShapeDtypeStruct((b, t, BW), BF16),
        jax.ShapeDtypeStruct((b, t, BW), BF16),
        jax.ShapeDtypeStruct((b, t, BW), F32),
        jax.ShapeDtypeStruct((b, t, BW), F32),
        jax.ShapeDtypeStruct((b, t, BW), F32),
        jax.ShapeDtypeStruct((b, nb, MOBA_BLOCK, BW), BF16),
        jax.ShapeDtypeStruct((b, nb, BW, MOBA_BLOCK), BF16),
        jax.ShapeDtypeStruct((b, nb, 1, BW), F32),
        jax.ShapeDtypeStruct((b, B_CONV - 1, BW), F32),
        jax.ShapeDtypeStruct((b, D_CONV - 1, BW), F32),
    )
    out_specs = (
        rows(), rows(), rows(), rows(), rows(), rows(), blocks(), blocks(),
        pl.BlockSpec((None, nbt, 1, BW), lambda i, j: (i, j, 0, 0)),
        pl.BlockSpec((None, B_CONV - 1, BW), lambda i, j: (i, 0, 0)),
        pl.BlockSpec((None, D_CONV - 1, BW), lambda i, j: (i, 0, 0)),
    )
    in_specs = [
        pl.BlockSpec((None, tm, d), lambda i, j: (i, j, 0)),
        full(w_in.shape), full(alng.shape), full(alnb.shape), full(aws.shape), full(abias.shape),
        full(bcw.shape), full(dcw.shape), full(dcb.shape), full(dlng.shape), full(dlnb.shape),
    ]
    return pl.pallas_call(
        functools.partial(_inmix_prompt_kernel, tm=tm, scale=LOG2E / math.sqrt(HEAD_DIM)),
        grid=(b, t // tm), in_specs=in_specs, out_specs=out_specs, out_shape=out_shape,
        scratch_shapes=[pltpu.VMEM((B_HALO + tm, BW), F32), pltpu.VMEM((D_HALO + tm, BW), F32)],
        compiler_params=_params("arbitrary", "arbitrary"), name="inmix_prompt",
    )(x, w_in, alng, alnb, aws, abias, bcw, dcw, dcb, dlng, dlnb)


def _inmix_sample_kernel(x_ref, w_ref, alng_ref, alnb_ref, w00_ref, b0_ref, bcw_ref, hb_ref, dcw_ref, dcb_ref,
                         hd_ref, dlng_ref, dlnb_ref,
                         ya_ref, yb_ref, yd_ref, q_ref, k_ref, v_ref, vn_ref, cbn_ref, cdn_ref, *, scale):
    x = x_ref[...]

    def proj(j):
        return _mm(x, w_ref[:, j * BW:(j + 1) * BW])

    vn = _ln(_gelu(proj(1)), alng_ref[...], alnb_ref[...])
    vn_ref[...] = vn
    ya_ref[...] = _gelu(proj(0)) * (vn * w00_ref[...] + b0_ref[...])

    cbn = proj(3) * proj(4)
    cbn_ref[...] = cbn
    conv = bcw_ref[B_CONV - 1:B_CONV, :] * cbn
    for kk in range(B_CONV - 1):
        conv = conv + bcw_ref[kk:kk + 1, :] * hb_ref[kk]
    yb_ref[...] = proj(2) * conv

    q_ref[...] = proj(5) * scale
    k_ref[...] = proj(6)
    v_ref[...] = proj(7)

    cdn = proj(8) * _sigmoid(proj(9))
    cdn_ref[...] = cdn
    conv = dcb_ref[...] + dcw_ref[D_CONV - 1:D_CONV, :] * cdn
    for kk in range(D_CONV - 1):
        conv = conv + dcw_ref[kk:kk + 1, :] * hd_ref[kk]
    yd_ref[...] = _silu(_ln(conv, dlng_ref[...], dlnb_ref[...]))


def _inmix_sample(x, w_in, alng, alnb, w00, b0, bcw, hb, dcw, dcb, hd, dlng, dlnb):
    s = x.shape[0]
    args = (x, w_in, alng, alnb, w00, b0, bcw, hb, dcw, dcb, hd, dlng, dlnb)
    full = lambda shape: pl.BlockSpec(shape, lambda i: (0,) * len(shape))
    row = lambda dt: jax.ShapeDtypeStruct((s, BW), dt)
    out_shape = tuple(row(F32) for _ in range(9))
    return pl.pallas_call(
        functools.partial(_inmix_sample_kernel, scale=1.0 / math.sqrt(HEAD_DIM)),
        grid=(1,), in_specs=[full(a.shape) for a in args], out_specs=tuple(full((s, BW)) for _ in out_shape),
        out_shape=out_shape, compiler_params=_params("arbitrary"), name="inmix_sample",
    )(*args)


def _attn_prompt_kernel(q_ref, kb_ref, vt_ref, km_ref, yc_ref, bias_scr, qh_scr, m_scr, l_scr, acc_scr, s_a, s_b,
                        *, nb):
    c = pl.program_id(1)
    qb = MOBA_BLOCK
    q = q_ref[...]
    q16 = q.astype(BF16)
    km = km_ref[...]
    lane_head_q = _lane_head((qb, BW))
    lane_head_km = _lane_head((nb, BW))
    blk = lax.broadcasted_iota(jnp.int32, (nb, qb), 0)
    key_pos = lax.broadcasted_iota(jnp.int32, (qb, qb), 0)
    q_pos = lax.broadcasted_iota(jnp.int32, (qb, qb), 1)
    valid = blk < c

    for h in range(HEADS):
        gate = _mm_nt(jnp.where(lane_head_km == h, km, 0.0), q, precision=HIGHEST)
        g = jnp.where(valid, gate, -jnp.inf)
        sel = jnp.zeros((nb, qb), F32)
        for _ in range(min(MOBA_TOPK, nb)):
            m = jnp.max(g, axis=0, keepdims=True)
            idx = jnp.min(jnp.where(g == m, blk, nb), axis=0, keepdims=True)
            hit = blk == idx
            sel = jnp.where(hit, 1.0, sel)
            g = jnp.where(hit, -jnp.inf, g)
        bias_scr[h] = jnp.where(jnp.logical_and(sel > 0.5, valid), 0.0, NEG)

        qh = jnp.where(lane_head_q == h, q16, jnp.zeros_like(q16))
        qh_scr[h * qb:(h + 1) * qb, :] = qh
        hs = slice(h * HEAD_DIM, (h + 1) * HEAD_DIM)
        s = jnp.where(key_pos <= q_pos, _mm_nt(kb_ref[c], qh), NEG)
        m = jnp.max(s, axis=0, keepdims=True)
        p = jnp.exp2(s - m)
        m_scr[h] = m
        l_scr[h] = jnp.sum(p, axis=0, keepdims=True)
        acc_scr[hs, :] = _mm(vt_ref[c, hs, :], p.astype(BF16))

    s_a[...] = _mm_nt(kb_ref[0], qh_scr[...])

    def step(j, src, dst):
        dst[...] = _mm_nt(kb_ref[jnp.minimum(j + 1, c - 1)], qh_scr[...])
        for h in range(HEADS):
            hs = slice(h * HEAD_DIM, (h + 1) * HEAD_DIM)
            s = src[:, h * qb:(h + 1) * qb] + bias_scr[h, pl.ds(j, 1), :]
            m = m_scr[h]
            m_new = jnp.maximum(m, jnp.max(s, axis=0, keepdims=True))
            a = jnp.exp2(m - m_new)
            p = jnp.exp2(s - m_new)
            m_scr[h] = m_new
            l_scr[h] = a * l_scr[h] + jnp.sum(p, axis=0, keepdims=True)
            acc_scr[hs, :] = a * acc_scr[hs, :] + _mm(vt_ref[j, hs, :], p.astype(BF16))

    def pair(i, carry):
        step(2 * i, s_a, s_b)
        step(2 * i + 1, s_b, s_a)
        return carry

    lax.fori_loop(0, lax.shift_right_logical(c, 1), pair, 0)

    @pl.when(lax.rem(c, 2) == 1)
    def _():
        step(c - 1, s_a, s_b)

    for h in range(HEADS):
        hs = slice(h * HEAD_DIM, (h + 1) * HEAD_DIM)
        acc_scr[hs, :] = acc_scr[hs, :] / l_scr[h]
    yc_ref[...] = acc_scr[...].T.astype(BF16)


def _attn_prompt(q, kb, vt, km):
    b, t, _ = q.shape
    nb = t // MOBA_BLOCK
    return pl.pallas_call(
        functools.partial(_attn_prompt_kernel, nb=nb),
        grid=(b, nb),
        in_specs=[
            pl.BlockSpec((None, MOBA_BLOCK, BW), lambda i, j: (i, j, 0)),
            pl.BlockSpec((None, nb, MOBA_BLOCK, BW), lambda i, j: (i, 0, 0, 0)),
            pl.BlockSpec((None, nb, BW, MOBA_BLOCK), lambda i, j: (i, 0, 0, 0)),
            pl.BlockSpec((None, nb, BW), lambda i, j: (i, 0, 0)),
        ],
        out_specs=pl.BlockSpec((None, MOBA_BLOCK, BW), lambda i, j: (i, j, 0)),
        out_shape=jax.ShapeDtypeStruct((b, t, BW), BF16),
        scratch_shapes=[pltpu.VMEM((HEADS, nb, MOBA_BLOCK), F32), pltpu.VMEM((HEADS * MOBA_BLOCK, BW), BF16),
                        pltpu.VMEM((HEADS, 1, MOBA_BLOCK), F32), pltpu.VMEM((HEADS, 1, MOBA_BLOCK), F32),
                        pltpu.VMEM((BW, MOBA_BLOCK), F32), pltpu.VMEM((MOBA_BLOCK, HEADS * MOBA_BLOCK), F32),
                        pltpu.VMEM((MOBA_BLOCK, HEADS * MOBA_BLOCK), F32)],
        compiler_params=_params("arbitrary", "arbitrary"), name="attn_prompt",
    )(q, kb, vt, km)


def _attn_sample_kernel(pt_ref, ck_ref, cv_ref, q_ref, kn_ref, vn_ref, o_ref, kbuf, vbuf, bsum, ksem, vsem,
                        *, layer, n_pages, ppb):
    b = pl.program_id(0)
    ns = pl.num_programs(0)
    slot = lax.rem(b, 2)
    nbp = n_pages // ppb
    per_head = MOBA_TOPK * ppb

    def k_copy(sample, i, sl):
        return pltpu.make_async_copy(ck_ref.at[layer, pt_ref[sample * n_pages + i]], kbuf.at[sl, i], ksem.at[sl])

    @pl.when(b == 0)
    def _():
        for i in range(n_pages):
            k_copy(0, i, 0).start()

    for i in range(n_pages):
        k_copy(b, i, slot).wait()

    @pl.when(b + 1 < ns)
    def _():
        for i in range(n_pages):
            k_copy(b + 1, i, 1 - slot).start()

    q = q_ref[...]
    group = SUBLANES

    def block_sums(jg, carry):
        rows = [[] for _ in range(HEADS)]
        for jj in range(group):
            tot = None
            for u in range(ppb):
                pr = kbuf[slot, (jg * group + jj) * ppb + u] * q
                tot = pr if tot is None else tot + pr
            for h in range(HEADS):
                rows[h].append(jnp.sum(tot[h * HEAD_DIM:(h + 1) * HEAD_DIM, :], axis=0, keepdims=True))
        for h in range(HEADS):
            bsum[h, pl.ds(pl.multiple_of(jg * group, group), group), :] = jnp.concatenate(rows[h], axis=0)
        return carry

    lax.fori_loop(0, nbp // group, block_sums, 0)

    blk = lax.broadcasted_iota(jnp.int32, (nbp, 1), 0)
    sel = []
    for h in range(HEADS):
        g = jnp.sum(bsum[h], axis=1, keepdims=True) * (1.0 / MOBA_BLOCK)
        for _ in range(MOBA_TOPK):
            idx = jnp.min(jnp.where(g == jnp.max(g), blk, nbp))
            sel.append(idx)
            g = jnp.where(blk == idx, -jnp.inf, g)

    def v_copy(h, r, u):
        n = h * per_head + r * ppb + u
        page = pt_ref[b * n_pages + sel[h * MOBA_TOPK + r] * ppb + u]
        return pltpu.make_async_copy(cv_ref.at[layer, page, pl.ds(h * HEAD_DIM, HEAD_DIM)], vbuf.at[n], vsem.at[0])

    v_copies = [v_copy(h, r, u) for h in range(HEADS) for r in range(MOBA_TOPK) for u in range(ppb)]
    for cp in v_copies:
        cp.start()

    kn = kn_ref[...]
    vn = vn_ref[...]
    probs = []
    for h in range(HEADS):
        hs = slice(h * HEAD_DIM, (h + 1) * HEAD_DIM)
        qh = q[hs, :]
        s_new = jnp.sum(qh * kn[hs, :], axis=0, keepdims=True)
        ss = []
        for r in range(MOBA_TOPK):
            for u in range(ppb):
                kp = kbuf[slot, sel[h * MOBA_TOPK + r] * ppb + u, hs, :]
                ss.append(jnp.sum(kp * qh, axis=0, keepdims=True))
        m = s_new
        for s in ss:
            m = jnp.maximum(m, jnp.max(s, axis=1, keepdims=True))
        p_new = jnp.exp(s_new - m)
        ps = [jnp.exp(s - m) for s in ss]
        l = p_new
        for p in ps:
            l = l + jnp.sum(p, axis=1, keepdims=True)
        probs.append((p_new, ps, l))

    for cp in v_copies:
        cp.wait()

    for h in range(HEADS):
        hs = slice(h * HEAD_DIM, (h + 1) * HEAD_DIM)
        p_new, ps, l = probs[h]
        o = p_new * vn[hs, :]
        for n, p in enumerate(ps):
            o = o + jnp.sum(vbuf[h * per_head + n] * p, axis=1, keepdims=True)
        o_ref[hs, :] = o / l


def _attn_sample(pt_flat, cache_k, cache_v, q, kn, vn, *, layer, n_pages, ppb):
    s = q.shape[0]
    page = cache_k.shape[3]
    n_sel = HEADS * MOBA_TOPK * ppb
    nbp = n_pages // ppb
    assert nbp % SUBLANES == 0
    col = lambda: pl.BlockSpec((None, BW, 1), lambda i, pt: (i, 0, 0))
    grid_spec = pltpu.PrefetchScalarGridSpec(
        num_scalar_prefetch=1, grid=(s,),
        in_specs=[pl.BlockSpec(memory_space=pl.ANY), pl.BlockSpec(memory_space=pl.ANY), col(), col(), col()],
        out_specs=col(),
        scratch_shapes=[pltpu.VMEM((2, n_pages, BW, page), F32), pltpu.VMEM((n_sel, HEAD_DIM, page), F32),
                        pltpu.VMEM((HEADS, nbp, page), F32),
                        pltpu.SemaphoreType.DMA((2,)), pltpu.SemaphoreType.DMA((1,))],
    )
    return pl.pallas_call(
        functools.partial(_attn_sample_kernel, layer=layer, n_pages=n_pages, ppb=ppb),
        grid_spec=grid_spec, out_shape=jax.ShapeDtypeStruct((s, BW, 1), F32),
        compiler_params=_params("arbitrary"), name="attn_sample",
    )(pt_flat, cache_k, cache_v, q[:, :, None], kn[:, :, None], vn[:, :, None])


def _merge_kernel(x_ref, ya_ref, yb_ref, yc_ref, yd_ref, wg_ref, bg_ref, wbr_ref, wo_ref, g_ref, b_ref, o_ref,
                  *, alpha):
    x = x_ref[...]
    xw = x.astype(wg_ref.dtype)
    d = x.shape[1]
    acc = None
    for j, y_ref in enumerate((ya_ref, yb_ref, yc_ref, yd_ref)):
        gate = _sigmoid(_mm(xw, wg_ref[:, j * d:(j + 1) * d]) + bg_ref[:, j * d:(j + 1) * d])
        term = gate * _mm(y_ref[...], wbr_ref[j])
        acc = term if acc is None else acc + term
    out = _mm(acc, wo_ref[...])
    o_ref[...] = _ln(alpha * x + out, g_ref[...], b_ref[...])


def _merge(x, ya, yb, yc, yd, wg, bg, wbr, wo, g, b, *, alpha):
    m, d = x.shape
    tm = _row_tile(m, 256)
    full = lambda a: pl.BlockSpec(a.shape, lambda i: (0,) * a.ndim, pipeline_mode=pl.Buffered(1))
    rows = lambda w: pl.BlockSpec((tm, w), lambda i: (i, 0))
    return pl.pallas_call(
        functools.partial(_merge_kernel, alpha=alpha),
        grid=(m // tm,),
        in_specs=[rows(d), rows(BW), rows(BW), rows(BW), rows(BW), full(wg), full(bg), full(wbr), full(wo),
                  full(g), full(b)],
        out_specs=rows(d), out_shape=jax.ShapeDtypeStruct((m, d), F32),
        compiler_params=_params("arbitrary"), name="merge",
    )(x, ya, yb, yc, yd, wg, bg, wbr, wo, g, b)


def _ffn_kernel(x_ref, wg_ref, wu_ref, wd_ref, g_ref, b_ref, o_ref, acc_ref, *, alpha):
    f = pl.program_id(1)

    @pl.when(f == 0)
    def _():
        acc_ref[...] = jnp.zeros_like(acc_ref)

    xw = x_ref[...].astype(wg_ref.dtype)
    h = _silu(_mm(xw, wg_ref[...])) * _mm(xw, wu_ref[...])
    acc_ref[...] += _mm(h, wd_ref[...])

    @pl.when(f == pl.num_programs(1) - 1)
    def _():
        o_ref[...] = _ln(alpha * x_ref[...] + acc_ref[...], g_ref[...], b_ref[...])


def _ffn_tile(dff):
    return next(c for c in (1408, 1024, 896, 768, 512, 384, 256, 128) if dff % c == 0)


def _ffn(x, wg, wu, wd, g, b, *, alpha):
    m, d = x.shape
    dff = wg.shape[1]
    tm = _row_tile(m, 512)
    tf = _ffn_tile(dff)
    full = lambda a: pl.BlockSpec(a.shape, lambda i, f: (0,) * a.ndim)
    return pl.pallas_call(
        functools.partial(_ffn_kernel, alpha=alpha),
        grid=(m // tm, dff // tf),
        in_specs=[pl.BlockSpec((tm, d), lambda i, f: (i, 0)), pl.BlockSpec((d, tf), lambda i, f: (0, f)),
                  pl.BlockSpec((d, tf), lambda i, f: (0, f)), pl.BlockSpec((tf, d), lambda i, f: (f, 0)),
                  full(g), full(b)],
        out_specs=pl.BlockSpec((tm, d), lambda i, f: (i, 0)), out_shape=jax.ShapeDtypeStruct((m, d), F32),
        scratch_shapes=[pltpu.VMEM((tm, d), F32)],
        compiler_params=_params("arbitrary", "arbitrary"), name="ffn",
    )(x, wg, wu, wd, g, b)


def _router_kernel(x_ref, wr_ref, br_ref, o_ref):
    lg = jnp.dot(x_ref[...], wr_ref[...], precision=HIGHEST, preferred_element_type=F32) + br_ref[...]
    o_ref[...] = _top2(lg)[0]


def _router(x, wr, br):
    m, d = x.shape
    tm = _row_tile(m, 512)
    full = lambda a: pl.BlockSpec(a.shape, lambda i: (0,) * a.ndim)
    return pl.pallas_call(
        _router_kernel, grid=(m // tm,),
        in_specs=[pl.BlockSpec((tm, d), lambda i: (i, 0)), full(wr), full(br)],
        out_specs=pl.BlockSpec((tm, LANES), lambda i: (i, 0)), out_shape=jax.ShapeDtypeStruct((m, LANES), F32),
        compiler_params=_params("arbitrary"), name="router",
    )(x, wr, br)


def _moe_kernel(x_ref, c_ref, wg_ref, wu_ref, wd_ref, g_ref, b_ref, o_ref, acc_ref, *, alpha):
    e = pl.program_id(1)
    f = pl.program_id(2)

    @pl.when(jnp.logical_and(e == 0, f == 0))
    def _():
        acc_ref[...] = jnp.zeros_like(acc_ref)

    comb = c_ref[...]
    lane = lax.broadcasted_iota(jnp.int32, comb.shape, 1)
    ce = jnp.sum(jnp.where(lane == e, comb, 0.0), axis=1, keepdims=True)
    xw = x_ref[...].astype(wg_ref.dtype)
    h = _silu(_mm(xw, wg_ref[...])) * _mm(xw, wu_ref[...])
    acc_ref[...] += ce * _mm(h, wd_ref[...])

    @pl.when(jnp.logical_and(e == pl.num_programs(1) - 1, f == pl.num_programs(2) - 1))
    def _():
        o_ref[...] = _ln(alpha * x_ref[...] + acc_ref[...], g_ref[...], b_ref[...])


def _moe(x, comb, wg, wu, wd, g, b, *, alpha):
    m, d = x.shape
    ne, _, dff = wg.shape
    tm = _row_tile(m, 512)
    tf = _ffn_tile(dff)
    full = lambda a: pl.BlockSpec(a.shape, lambda i, e, f: (0,) * a.ndim)
    return pl.pallas_call(
        functools.partial(_moe_kernel, alpha=alpha),
        grid=(m // tm, ne, dff // tf),
        in_specs=[pl.BlockSpec((tm, d), lambda i, e, f: (i, 0)), pl.BlockSpec((tm, LANES), lambda i, e, f: (i, 0)),
                  pl.BlockSpec((None, d, tf), lambda i, e, f: (e, 0, f)),
                  pl.BlockSpec((None, d, tf), lambda i, e, f: (e, 0, f)),
                  pl.BlockSpec((None, tf, d), lambda i, e, f: (e, f, 0)), full(g), full(b)],
        out_specs=pl.BlockSpec((tm, d), lambda i, e, f: (i, 0)), out_shape=jax.ShapeDtypeStruct((m, d), F32),
        scratch_shapes=[pltpu.VMEM((tm, d), F32)],
        compiler_params=_params("arbitrary", "arbitrary", "arbitrary"), name="moe",
    )(x, comb, wg, wu, wd, g, b)


MOE_TILE = 512
FLAG_VALID, FLAG_FIRST, FLAG_LAST = 1, 2, 4


def _top2(lg):
    lane = lax.broadcasted_iota(jnp.int32, lg.shape, 1)
    m1 = jnp.max(lg, axis=1, keepdims=True)
    hit1 = lane == jnp.min(jnp.where(lg == m1, lane, LANES), axis=1, keepdims=True)
    lg2 = jnp.where(hit1, -jnp.inf, lg)
    m2 = jnp.max(lg2, axis=1, keepdims=True)
    hit2 = lane == jnp.min(jnp.where(lg2 == m2, lane, LANES), axis=1, keepdims=True)
    e = jnp.exp(m2 - m1)
    comb = jnp.where(hit1, 1.0 / (1.0 + e), 0.0) + jnp.where(hit2, e / (1.0 + e), 0.0)
    return comb, jnp.logical_or(hit1, hit2)


def _route_kernel(x_ref, wr_ref, br_ref, comb_ref, rank_ref, rankt_ref, cum_ref, carry):
    @pl.when(pl.program_id(0) == 0)
    def _():
        carry[...] = jnp.zeros_like(carry)

    lg = jnp.dot(x_ref[...], wr_ref[...], precision=HIGHEST, preferred_element_type=F32) + br_ref[...]
    comb, chosen = _top2(lg)
    comb_ref[...] = comb
    sel = jnp.where(chosen, 1.0, 0.0)
    tc = sel.shape[0]
    before = lax.broadcasted_iota(jnp.int32, (tc, tc), 0) > lax.broadcasted_iota(jnp.int32, (tc, tc), 1)
    prefix = _mm(jnp.where(before, 1.0, 0.0).astype(BF16), sel.astype(BF16))
    rank = jnp.where(chosen, prefix + carry[0:1, :], -1.0)
    rank_ref[...] = rank
    rankt_ref[...] = rank.T[0:SUBLANES, :]
    carry[...] = carry[...] + jnp.sum(sel, axis=0, keepdims=True)
    cum_ref[...] = carry[...]


def _route(x, wr, br):
    m, d = x.shape
    tc = MOE_TILE
    assert m % tc == 0 and N_EXPERTS <= SUBLANES
    n_ck = m // tc
    full = lambda a: pl.BlockSpec(a.shape, lambda i: (0,) * a.ndim)
    return pl.pallas_call(
        _route_kernel, grid=(n_ck,),
        in_specs=[pl.BlockSpec((tc, d), lambda i: (i, 0)), full(wr), full(br)],
        out_specs=(pl.BlockSpec((tc, LANES), lambda i: (i, 0)), pl.BlockSpec((tc, LANES), lambda i: (i, 0)),
                   pl.BlockSpec((SUBLANES, tc), lambda i: (0, i)),
                   pl.BlockSpec((None, SUBLANES, LANES), lambda i: (i, 0, 0))),
        out_shape=(jax.ShapeDtypeStruct((m, LANES), F32), jax.ShapeDtypeStruct((m, LANES), F32),
                   jax.ShapeDtypeStruct((SUBLANES, m), F32), jax.ShapeDtypeStruct((n_ck, SUBLANES, LANES), F32)),
        scratch_shapes=[pltpu.VMEM((SUBLANES, LANES), F32)],
        compiler_params=_params("arbitrary"), name="route",
    )(x, wr, br)


def _moe_plan(cum_incl, *, n_ck, nt, ni):
    ne, tf = N_EXPERTS, MOE_TILE
    i32 = jnp.int32
    ci = cum_incl[:, 0, :ne].astype(i32)
    ce = jnp.concatenate([jnp.zeros((1, ne), i32), ci[:-1]], axis=0)
    cnt = ci - ce
    gsz = ((ci[-1] + tf - 1) // tf) * tf
    gend = jnp.cumsum(gsz)
    goff = gend - gsz
    tiles = jnp.arange(nt, dtype=i32) * tf
    te = jnp.minimum(jnp.searchsorted(gend, tiles, side="right"), ne - 1).astype(i32)
    tv = (tiles < gend[-1]).astype(i32)

    a = (goff[None, :] + ce).T.reshape(-1)
    c = cnt.T.reshape(-1)
    t_lo = a // tf
    nrep = jnp.where(c > 0, (a + c - 1) // tf - t_lo + 1, 0)
    n_items = jnp.sum(nrep)
    start = jnp.cumsum(nrep) - nrep
    pair = jnp.repeat(jnp.arange(ne * n_ck, dtype=i32), nrep, total_repeat_length=ni)
    w = jnp.arange(ni, dtype=i32)
    valid = w < n_items
    it = t_lo[pair] + (w - start[pair])
    ick = pair % n_ck
    ie = pair // n_ck

    def finish(it, ick, ie, valid, by_tile):
        last = n_items - 1
        spare = gend[-1] // tf + (w - n_items)
        fill = by_tile & ~valid & (spare < nt)
        it = jnp.where(valid, it, jnp.minimum(spare, nt - 1) if by_tile else it[last])
        ick, ie = (jnp.where(valid, v, v[last]) for v in (ick, ie))
        g = it if by_tile else ick
        first = valid & (g != jnp.concatenate([jnp.full((1,), -1, i32), g[:-1]]))
        final = valid & ((g != jnp.concatenate([g[1:], jnp.full((1,), -1, i32)])) | (w == last))
        flags = valid * FLAG_VALID + (first | fill) * FLAG_FIRST + (final | fill) * FLAG_LAST
        return it.astype(i32), ick.astype(i32), ie.astype(i32), flags.astype(i32)

    by_tile = finish(it, ick, ie, valid, True)
    order = jnp.argsort(jnp.where(valid, ick * nt + it, jnp.iinfo(i32).max))
    by_chunk = finish(it[order], ick[order], ie[order], valid[order], False)
    return goff.astype(i32), te, tv, by_tile, by_chunk


def _moe_gather_kernel(t_ref, ck_ref, e_ref, flag_ref, goff_ref, x_ref, rankt_ref, xs_ref, acc_ref):
    w = pl.program_id(0)
    flag = flag_ref[w]
    tf, tc = acc_ref.shape[0], x_ref.shape[0]

    @pl.when((flag & FLAG_FIRST) != 0)
    def _():
        acc_ref[...] = jnp.zeros_like(acc_ref)

    @pl.when((flag & FLAG_VALID) != 0)
    def _():
        e = e_ref[w]
        shift = (goff_ref[e] - t_ref[w] * tf).astype(F32)
        dest = rankt_ref[pl.ds(e, 1), :] + shift
        row = lax.broadcasted_iota(jnp.int32, (tf, tc), 0).astype(F32)
        onehot = jnp.where(row == dest, 1.0, 0.0).astype(BF16)
        acc_ref[...] += _mm(onehot, x_ref[...].astype(BF16))

    @pl.when((flag & FLAG_LAST) != 0)
    def _():
        xs_ref[...] = acc_ref[...].astype(BF16)


def _moe_gather(plan, goff, x, rankt, *, nt):
    m, d = x.shape
    tf = tc = MOE_TILE
    ni = plan[0].shape[0]
    grid_spec = pltpu.PrefetchScalarGridSpec(
        num_scalar_prefetch=5, grid=(ni,),
        in_specs=[pl.BlockSpec((tc, d), lambda w, t, ck, e, fl, go: (ck[w], 0)),
                  pl.BlockSpec((SUBLANES, tc), lambda w, t, ck, e, fl, go: (0, ck[w]))],
        out_specs=pl.BlockSpec((tf, d), lambda w, t, ck, e, fl, go: (t[w], 0)),
        scratch_shapes=[pltpu.VMEM((tf, d), F32)],
    )
    return pl.pallas_call(
        _moe_gather_kernel, grid_spec=grid_spec, out_shape=jax.ShapeDtypeStruct((nt * tf, d), BF16),
        compiler_params=_params("arbitrary"), name="moe_gather",
    )(*plan, goff, x, rankt)


def _moe_ffn_kernel(te_ref, tv_ref, xs_ref, wg_ref, wu_ref, wd_ref, ys_ref, acc_ref):
    t = pl.program_id(0)
    f = pl.program_id(1)

    @pl.when(f == 0)
    def _():
        acc_ref[...] = jnp.zeros_like(acc_ref)

    @pl.when(tv_ref[t] != 0)
    def _():
        xs = xs_ref[...]
        h = _silu(_mm(xs, wg_ref[...])) * _mm(xs, wu_ref[...])
        acc_ref[...] += _mm(h, wd_ref[...])

    @pl.when(f == pl.num_programs(1) - 1)
    def _():
        ys_ref[...] = acc_ref[...].astype(BF16)


def _moe_ffn(te, tv, xs, wg, wu, wd):
    r, d = xs.shape
    dff = wg.shape[2]
    tf = MOE_TILE
    tff = _ffn_tile(dff)
    grid_spec = pltpu.PrefetchScalarGridSpec(
        num_scalar_prefetch=2, grid=(r // tf, dff // tff),
        in_specs=[pl.BlockSpec((tf, d), lambda t, f, te, tv: (t, 0)),
                  pl.BlockSpec((None, d, tff), lambda t, f, te, tv: (te[t], 0, f)),
                  pl.BlockSpec((None, d, tff), lambda t, f, te, tv: (te[t], 0, f)),
                  pl.BlockSpec((None, tff, d), lambda t, f, te, tv: (te[t], f, 0))],
        out_specs=pl.BlockSpec((tf, d), lambda t, f, te, tv: (t, 0)),
        scratch_shapes=[pltpu.VMEM((tf, d), F32)],
    )
    return pl.pallas_call(
        _moe_ffn_kernel, grid_spec=grid_spec, out_shape=jax.ShapeDtypeStruct((r, d), BF16),
        compiler_params=_params("arbitrary", "arbitrary"), name="moe_ffn",
    )(te, tv, xs, wg, wu, wd)


def _moe_combine_kernel(t_ref, ck_ref, e_ref, flag_ref, goff_ref, ys_ref, rank_ref, comb_ref, x_ref, g_ref, b_ref,
                        o_ref, acc_ref, *, alpha):
    w = pl.program_id(0)
    flag = flag_ref[w]
    tf, tc = ys_ref.shape[0], x_ref.shape[0]

    @pl.when((flag & FLAG_FIRST) != 0)
    def _():
        acc_ref[...] = jnp.zeros_like(acc_ref)

    @pl.when((flag & FLAG_VALID) != 0)
    def _():
        e = e_ref[w]
        shift = (goff_ref[e] - t_ref[w] * tf).astype(F32)
        mine = lax.broadcasted_iota(jnp.int32, (tc, LANES), 1) == e
        dest = jnp.sum(jnp.where(mine, rank_ref[...], 0.0), axis=1, keepdims=True) + shift
        weight = jnp.sum(jnp.where(mine, comb_ref[...], 0.0), axis=1, keepdims=True)
        col = lax.broadcasted_iota(jnp.int32, (tc, tf), 1).astype(F32)
        onehot = jnp.where(col == dest, 1.0, 0.0).astype(BF16)
        acc_ref[...] += weight * _mm(onehot, ys_ref[...])

    @pl.when((flag & FLAG_LAST) != 0)
    def _():
        o_ref[...] = _ln(alpha * x_ref[...] + acc_ref[...], g_ref[...], b_ref[...])


def _moe_combine(plan, goff, ys, rank, comb, x, g, b, *, alpha):
    m, d = x.shape
    tf = tc = MOE_TILE
    ni = plan[0].shape[0]
    chunk = lambda width: pl.BlockSpec((tc, width), lambda w, t, ck, e, fl, go: (ck[w], 0))
    full = lambda a: pl.BlockSpec(a.shape, lambda w, t, ck, e, fl, go: (0,) * a.ndim)
    grid_spec = pltpu.PrefetchScalarGridSpec(
        num_scalar_prefetch=5, grid=(ni,),
        in_specs=[pl.BlockSpec((tf, d), lambda w, t, ck, e, fl, go: (t[w], 0)), chunk(LANES), chunk(LANES), chunk(d),
                  full(g), full(b)],
        out_specs=chunk(d),
        scratch_shapes=[pltpu.VMEM((tc, d), F32)],
    )
    return pl.pallas_call(
        functools.partial(_moe_combine_kernel, alpha=alpha), grid_spec=grid_spec,
        out_shape=jax.ShapeDtypeStruct((m, d), F32),
        compiler_params=_params("arbitrary"), name="moe_combine",
    )(*plan, goff, ys, rank, comb, x, g, b)


def _moe_routed(x, wr, br, wg, wu, wd, g, b, *, alpha):
    m = x.shape[0]
    tf = MOE_TILE
    n_ck = m // tf
    nt = -(-(2 * m + N_EXPERTS * (tf - 1)) // tf)
    ni = nt + N_EXPERTS * n_ck
    comb, rank, rankt, cum = _route(x, wr, br)
    goff, te, tv, by_tile, by_chunk = _moe_plan(cum, n_ck=n_ck, nt=nt, ni=ni)
    xs = _moe_gather(by_tile, goff, x, rankt, nt=nt)
    ys = _moe_ffn(te, tv, xs, wg, wu, wd)
    return _moe_combine(by_chunk, goff, ys, rank, comb, x, g, b, alpha=alpha)


def _ple_kernel(x_ref, p_ref, wg_ref, wp_ref, g_ref, b_ref, o_ref, *, alpha):
    x = x_ref[...]
    gate = _sigmoid(_mm(x, wg_ref[...]))
    o_ref[...] = _ln(alpha * x + gate * _mm(p_ref[...], wp_ref[...]), g_ref[...], b_ref[...])


def _ple(x, p, wg, wp, g, b, *, alpha):
    m, d = x.shape
    tm = _row_tile(m, 512)
    full = lambda a: pl.BlockSpec(a.shape, lambda i: (0,) * a.ndim)
    return pl.pallas_call(
        functools.partial(_ple_kernel, alpha=alpha),
        grid=(m // tm,),
        in_specs=[pl.BlockSpec((tm, d), lambda i: (i, 0)), pl.BlockSpec((tm, p.shape[1]), lambda i: (i, 0)),
                  full(wg), full(wp), full(g), full(b)],
        out_specs=pl.BlockSpec((tm, d), lambda i: (i, 0)), out_shape=jax.ShapeDtypeStruct((m, d), F32),
        compiler_params=_params("arbitrary"), name="ple",
    )(x, p, wg, wp, g, b)


def kernel(x_prompt, x_sample, p_prompt, p_sample, cache_k, cache_v, state_conv_b, state_conv_d, page_table, w_in, w_gate, b_gate, a_ln_g, a_ln_b, a_w_s, a_b_s, b_conv_w, d_conv_w, d_conv_b, d_ln_g, d_ln_b, w_branch, w_out, ln_g, ln_b, ffn_w_gate, ffn_w_up, ffn_w_down, moe_w_router, moe_b_router, moe_w_gate, moe_w_up, moe_w_down, ple_w_gate, ple_w_proj):
    depth = w_in.shape[0]
    alpha = (2 * depth) ** 0.25
    nbatch, seq, d = x_prompt.shape
    ns, dec_seq, _ = x_sample.shape
    assert dec_seq == 1, "the sample group decodes one token per sequence"
    n_pool, page = cache_k.shape[1], cache_k.shape[2]
    n_pages = page_table.shape[1]
    assert MOBA_BLOCK % page == 0 and (n_pages * page) % MOBA_BLOCK == 0 and (n_pages * page) % CHUNK == 0
    ppb = MOBA_BLOCK // page
    nbp = n_pages // ppb
    assert nbp >= MOBA_TOPK
    pt_flat = page_table.reshape(-1).astype(jnp.int32)
    ck_all = jnp.transpose(cache_k, (0, 1, 3, 4, 2)).reshape(depth, n_pool, BW, page)
    cv_all = jnp.transpose(cache_v, (0, 1, 3, 4, 2)).reshape(depth, n_pool, BW, page)
    row2 = lambda a: a.reshape(1, -1)

    xp = x_prompt
    xs = x_sample.reshape(ns, d)
    outs = {k: [] for k in ("kp", "vp", "ks", "vs", "cbp", "cbs", "cdp", "cds", "cvs")}
    for i in range(depth):
        w_in_i = w_in[i].astype(BF16)
        w_gate_i = w_gate[i].astype(BF16)
        w_branch_i = w_branch[i].astype(BF16)
        w_out_i = w_out[i].astype(BF16)
        alng, alnb, dlng, dlnb, dcb = (row2(a[i]) for a in (a_ln_g, a_ln_b, d_ln_g, d_ln_b, d_conv_b))
        bg = row2(b_gate[i])

        abias = jnp.repeat(a_b_s[i].T, HEAD_DIM, axis=1)
        ya, yb, yd, q, k, v, kb, vt, km, tb, td = _inmix_prompt(
            xp, w_in_i, alng, alnb, a_w_s[i], abias, b_conv_w[i], d_conv_w[i], dcb, dlng, dlnb)
        yc = _attn_prompt(q, kb, vt, km.reshape(nbatch, seq // MOBA_BLOCK, BW))
        outs["kp"].append(k.reshape(nbatch, seq, HEADS, HEAD_DIM))
        outs["vp"].append(v.reshape(nbatch, seq, HEADS, HEAD_DIM))
        outs["cbp"].append(tb)
        outs["cdp"].append(td)
        m = nbatch * seq
        x1p = _merge(xp.reshape(m, d), ya.reshape(m, BW), yb.reshape(m, BW), yc.reshape(m, BW), yd.reshape(m, BW),
                     w_gate_i, bg, w_branch_i, w_out_i, row2(ln_g[i, 0]), row2(ln_b[i, 0]), alpha=alpha)

        w00 = row2(jnp.repeat(a_w_s[i][:, 0, 0], HEAD_DIM))
        b0 = row2(jnp.repeat(a_b_s[i][:, 0], HEAD_DIM))
        hb = jnp.transpose(state_conv_b[i], (1, 0, 2))
        hd = jnp.transpose(state_conv_d[i], (1, 0, 2))
        sya, syb, syd, sq, sk, sv, svn, scbn, scdn = _inmix_sample(
            xs, w_in[i], alng, alnb, w00, b0, b_conv_w[i], hb, d_conv_w[i], dcb, hd, dlng, dlnb)
        syc = _attn_sample(pt_flat, ck_all, cv_all, sq, sk, sv, layer=i, n_pages=n_pages, ppb=ppb)
        outs["ks"].append(sk.reshape(ns, 1, HEADS, HEAD_DIM))
        outs["vs"].append(sv.reshape(ns, 1, HEADS, HEAD_DIM))
        outs["cbs"].append(jnp.concatenate([state_conv_b[i][:, 1:], scbn[:, None]], axis=1))
        outs["cds"].append(jnp.concatenate([state_conv_d[i][:, 1:], scdn[:, None]], axis=1))
        outs["cvs"].append(svn.reshape(ns, 1, BW))
        x1s = _merge(xs, sya, syb, syc.reshape(ns, BW), syd,
                     w_gate[i], bg, w_branch[i], w_out[i], row2(ln_g[i, 0]), row2(ln_b[i, 0]), alpha=alpha)

        g1, b1 = row2(ln_g[i, 1]), row2(ln_b[i, 1])
        j = i // 2
        if i % 2 == 0:
            wg, wu, wd = (w[j].astype(BF16) for w in (ffn_w_gate, ffn_w_up, ffn_w_down))
            x2p = _ffn(x1p, wg, wu, wd, g1, b1, alpha=alpha)
            x2s = _ffn(x1s, ffn_w_gate[j], ffn_w_up[j], ffn_w_down[j], g1, b1, alpha=alpha)
        else:
            wg, wu, wd = (w[j].astype(BF16) for w in (moe_w_gate, moe_w_up, moe_w_down))
            ne = moe_w_router.shape[2]
            wr = jnp.pad(moe_w_router[j], ((0, 0), (0, LANES - ne)))
            br = row2(jnp.pad(moe_b_router[j], (0, LANES - ne), constant_values=NEG))
            x2p = _moe_routed(x1p, wr, br, wg, wu, wd, g1, b1, alpha=alpha)
            x2s = _moe(x1s, _router(x1s, wr, br), wg, wu, wd, g1, b1, alpha=alpha)

        wpg = ple_w_gate[i].astype(BF16)
        wpp = ple_w_proj[i].astype(BF16)
        g2, b2 = row2(ln_g[i, 2]), row2(ln_b[i, 2])
        xp = _ple(x2p, p_prompt[i].reshape(m, -1), wpg, wpp, g2, b2, alpha=alpha).reshape(nbatch, seq, d)
        xs = _ple(x2s, p_sample[i].reshape(ns, -1), ple_w_gate[i], ple_w_proj[i], g2, b2, alpha=alpha)

    st = lambda key: jnp.stack(outs[key])
    return (xp, xs.reshape(ns, 1, d), st("kp"), st("vp"), st("ks"), st("vs"), st("cbp"), st("cbs"), st("cdp"),
            st("cds"), st("cvs"))
```

```python
import functools
import math

import jax
import jax.numpy as jnp
from jax import lax
from jax.experimental import pallas as pl
from jax.experimental.pallas import tpu as pltpu

F32 = jnp.float32
BF16 = jnp.bfloat16
HIGHEST = lax.Precision.HIGHEST

BW = 256
N_BRANCH = 4
CHUNK = 128
HEADS = 4
HEAD_DIM = BW // HEADS
HEAD_SHIFT = 6
MOBA_BLOCK = 256
MOBA_TOPK = 3
B_CONV = 3
D_CONV = 31
N_EXPERTS = 8
LN_EPS = 1e-5
NEG = -1e30
LOG2E = math.log2(math.e)
LANES = 128
SUBLANES = 8
V7X_VMEM_LIMIT = 56 * 1024 * 1024
CONV_ROWS = 64
B_HALO = 8
D_HALO = 32


def _params(*sem):
    return pltpu.CompilerParams(dimension_semantics=sem, vmem_limit_bytes=V7X_VMEM_LIMIT)


def _ln(x, g, b):
    mu = jnp.mean(x, axis=-1, keepdims=True)
    xc = x - mu
    var = jnp.mean(xc * xc, axis=-1, keepdims=True)
    return xc * lax.rsqrt(var + LN_EPS) * g + b


def _gelu(x):
    c = math.sqrt(2.0 / math.pi)
    return x * (0.5 * (1.0 + jnp.tanh(c * (x + 0.044715 * (x * x * x)))))


def _sigmoid(x):
    return 1.0 / (1.0 + jnp.exp(-x))


def _silu(x):
    return x * _sigmoid(x)


def _mm(a, b):
    if b.dtype == F32:
        return jnp.dot(a.astype(F32), b, precision=HIGHEST, preferred_element_type=F32)
    return jnp.dot(a.astype(BF16), b, preferred_element_type=F32)


def _mm_nt(a, b, **kw):
    return lax.dot_general(a, b, (((1,), (1,)), ((), ())), preferred_element_type=F32, **kw)


def _lane_head(shape):
    return lax.shift_right_logical(lax.broadcasted_iota(jnp.int32, shape, len(shape) - 1), HEAD_SHIFT)


def _row_tile(m, want):
    tm = min(want, m)
    assert m % tm == 0, (m, tm)
    return tm


def _inmix_prompt_kernel(x_ref, w_ref, alng_ref, alnb_ref, aws_ref, abias_ref, bcw_ref, dcw_ref, dcb_ref,
                         dlng_ref, dlnb_ref,
                         ya_ref, yb_ref, yd_ref, q_ref, k_ref, v_ref, kb_ref, vt_ref, km_ref, tb_ref, td_ref,
                         cb_buf, cd_buf, *, tm, scale):
    t = pl.program_id(1)
    last = pl.num_programs(1) - 1

    @pl.when(t == 0)
    def _():
        cb_buf[0:B_HALO, :] = jnp.zeros((B_HALO, BW), F32)
        cd_buf[0:D_HALO, :] = jnp.zeros((D_HALO, BW), F32)
        cd_buf[D_HALO + tm:D_HALO + tm + SUBLANES, :] = jnp.zeros((SUBLANES, BW), F32)

    xb = x_ref[...].astype(BF16)

    def proj(j):
        return _mm(xb, w_ref[:, j * BW:(j + 1) * BW])

    gu = _gelu(proj(0))
    vnb = _ln(_gelu(proj(1)), alng_ref[...], alnb_ref[...]).astype(BF16)
    row = lax.broadcasted_iota(jnp.int32, (CHUNK, CHUNK), 0)
    col = lax.broadcasted_iota(jnp.int32, (CHUNK, CHUNK), 1)
    wts = [jnp.where(row >= col, aws_ref[h], 0.0).astype(BF16) for h in range(HEADS)]
    lane_head = _lane_head((CHUNK, BW))
    for c in range(tm // CHUNK):
        sl = slice(c * CHUNK, (c + 1) * CHUNK)
        vc = vnb[sl, :]
        s = abias_ref[...]
        for h in range(HEADS):
            s = s + _mm(wts[h], jnp.where(lane_head == h, vc, jnp.zeros_like(vc)))
        ya_ref[sl, :] = (gu[sl, :] * s).astype(BF16)

    bb = proj(2)
    cb_buf[B_HALO:B_HALO + tm, :] = proj(3) * proj(4)
    off_b = B_HALO - (B_CONV - 1)
    for r0 in range(0, tm, CONV_ROWS):
        acc = jnp.zeros((CONV_ROWS, BW), F32)
        for kk in range(B_CONV):
            acc = acc + bcw_ref[kk:kk + 1, :] * cb_buf[r0 + kk + off_b:r0 + kk + off_b + CONV_ROWS, :]
        yb_ref[r0:r0 + CONV_ROWS, :] = (bb[r0:r0 + CONV_ROWS, :] * acc).astype(BF16)

    @pl.when(t == last)
    def _():
        tb_ref[...] = cb_buf[tm + off_b:tm + B_HALO, :]

    cb_buf[0:B_HALO, :] = cb_buf[tm:tm + B_HALO, :]

    q_ref[...] = proj(5) * scale
    k = proj(6)
    v = proj(7)
    k_ref[...] = k
    v_ref[...] = v
    for i in range(tm // MOBA_BLOCK):
        sl = slice(i * MOBA_BLOCK, (i + 1) * MOBA_BLOCK)
        kblk = k[sl, :]
        kb_ref[i] = kblk.astype(BF16)
        vt_ref[i] = v[sl, :].T.astype(BF16)
        km_ref[i] = jnp.sum(kblk, axis=0, keepdims=True) * (1.0 / MOBA_BLOCK)

    cd_buf[D_HALO:D_HALO + tm, :] = proj(8) * _sigmoid(proj(9))
    off_d = D_HALO - (D_CONV - 1)
    span = CONV_ROWS + D_HALO + SUBLANES
    for r0 in range(0, tm, CONV_ROWS):
        acc = jnp.zeros((CONV_ROWS, BW), F32) + dcb_ref[...]
        block = cd_buf[r0:r0 + span, :]
        for r in range(SUBLANES):
            win = block if r == 0 else pltpu.roll(block, span - r, 0)
            for off in range(r, D_HALO + 1, SUBLANES):
                kk = off - off_d
                if 0 <= kk < D_CONV:
                    acc = acc + dcw_ref[kk:kk + 1, :] * win[off - r:off - r + CONV_ROWS, :]
        yd_ref[r0:r0 + CONV_ROWS, :] = _silu(_ln(acc, dlng_ref[...], dlnb_ref[...])).astype(BF16)

    @pl.when(t == last)
    def _():
        td_ref[...] = cd_buf[tm + off_d:tm + D_HALO, :]

    cd_buf[0:D_HALO, :] = cd_buf[tm:tm + D_HALO, :]


def _inmix_prompt(x, w_in, alng, alnb, aws, abias, bcw, dcw, dcb, dlng, dlnb):
    b, t, d = x.shape
    tm = _row_tile(t, 512)
    assert tm % MOBA_BLOCK == 0 and tm % CONV_ROWS == 0
    nbt = tm // MOBA_BLOCK
    nb = t // MOBA_BLOCK
    full = lambda shape: pl.BlockSpec(shape, lambda i, j: (0,) * len(shape))
    rows = lambda: pl.BlockSpec((None, tm, BW), lambda i, j: (i, j, 0))
    blocks = lambda: pl.BlockSpec((None, nbt, MOBA_BLOCK, MOBA_BLOCK), lambda i, j: (i, j, 0, 0))
    out_shape = (
        jax.ShapeDtypeStruct((b, t, BW), BF16),
        jax.ShapeDtypeStruct((b, t, BW), BF16),
        jax.ShapeDtypeStruct((b, t, BW), BF16),
        jax.ShapeDtypeStruct((b, t, BW), F32),
        jax.ShapeDtypeStruct((b, t, BW), F32),
        jax.ShapeDtypeStruct((b, t, BW), F32),
        jax.ShapeDtypeStruct((b, nb, MOBA_BLOCK, BW), BF16),
        jax.ShapeDtypeStruct((b, nb, BW, MOBA_BLOCK), BF16),
        jax.ShapeDtypeStruct((b, nb, 1, BW), F32),
        jax.ShapeDtypeStruct((b, B_CONV - 1, BW), F32),
        jax.ShapeDtypeStruct((b, D_CONV - 1, BW), F32),
    )
    out_specs = (
        rows(), rows(), rows(), rows(), rows(), rows(), blocks(), blocks(),
        pl.BlockSpec((None, nbt, 1, BW), lambda i, j: (i, j, 0, 0)),
        pl.BlockSpec((None, B_CONV - 1, BW), lambda i, j: (i, 0, 0)),
        pl.BlockSpec((None, D_CONV - 1, BW), lambda i, j: (i, 0, 0)),
    )
    in_specs = [
        pl.BlockSpec((None, tm, d), lambda i, j: (i, j, 0)),
        full(w_in.shape), full(alng.shape), full(alnb.shape), full(aws.shape), full(abias.shape),
        full(bcw.shape), full(dcw.shape), full(dcb.shape), full(dlng.shape), full(dlnb.shape),
    ]
    return pl.pallas_call(
        functools.partial(_inmix_prompt_kernel, tm=tm, scale=LOG2E / math.sqrt(HEAD_DIM)),
        grid=(b, t // tm), in_specs=in_specs, out_specs=out_specs, out_shape=out_shape,
        scratch_shapes=[pltpu.VMEM((B_HALO + tm, BW), F32), pltpu.VMEM((D_HALO + tm + SUBLANES, BW), F32)],
        compiler_params=_params("arbitrary", "arbitrary"), name="inmix_prompt",
    )(x, w_in, alng, alnb, aws, abias, bcw, dcw, dcb, dlng, dlnb)


def _inmix_sample_kernel(x_ref, w_ref, alng_ref, alnb_ref, w00_ref, b0_ref, bcw_ref, hb_ref, dcw_ref, dcb_ref,
                         hd_ref, dlng_ref, dlnb_ref,
                         ya_ref, yb_ref, yd_ref, q_ref, k_ref, v_ref, vn_ref, cbn_ref, cdn_ref, *, scale):
    x = x_ref[...]

    def proj(j):
        return _mm(x, w_ref[:, j * BW:(j + 1) * BW])

    vn = _ln(_gelu(proj(1)), alng_ref[...], alnb_ref[...])
    vn_ref[...] = vn
    ya_ref[...] = _gelu(proj(0)) * (vn * w00_ref[...] + b0_ref[...])

    cbn = proj(3) * proj(4)
    cbn_ref[...] = cbn
    conv = bcw_ref[B_CONV - 1:B_CONV, :] * cbn
    for kk in range(B_CONV - 1):
        conv = conv + bcw_ref[kk:kk + 1, :] * hb_ref[kk]
    yb_ref[...] = proj(2) * conv

    q_ref[...] = proj(5) * scale
    k_ref[...] = proj(6)
    v_ref[...] = proj(7)

    cdn = proj(8) * _sigmoid(proj(9))
    cdn_ref[...] = cdn
    conv = dcb_ref[...] + dcw_ref[D_CONV - 1:D_CONV, :] * cdn
    for kk in range(D_CONV - 1):
        conv = conv + dcw_ref[kk:kk + 1, :] * hd_ref[kk]
    yd_ref[...] = _silu(_ln(conv, dlng_ref[...], dlnb_ref[...]))


def _inmix_sample(x, w_in, alng, alnb, w00, b0, bcw, hb, dcw, dcb, hd, dlng, dlnb):
    s = x.shape[0]
    args = (x, w_in, alng, alnb, w00, b0, bcw, hb, dcw, dcb, hd, dlng, dlnb)
    full = lambda shape: pl.BlockSpec(shape, lambda i: (0,) * len(shape))
    row = lambda dt: jax.ShapeDtypeStruct((s, BW), dt)
    out_shape = tuple(row(F32) for _ in range(9))
    return pl.pallas_call(
        functools.partial(_inmix_sample_kernel, scale=1.0 / math.sqrt(HEAD_DIM)),
        grid=(1,), in_specs=[full(a.shape) for a in args], out_specs=tuple(full((s, BW)) for _ in out_shape),
        out_shape=out_shape, compiler_params=_params("arbitrary"), name="inmix_sample",
    )(*args)


def _attn_prompt_kernel(q_ref, kb_ref, vt_ref, km_ref, yc_ref, bias_scr, qh_scr, m_scr, l_scr, acc_scr, s_a, s_b,
                        *, nb):
    c = pl.program_id(1)
    qb = MOBA_BLOCK
    q = q_ref[...]
    q16 = q.astype(BF16)
    km = km_ref[...]
    lane_head_q = _lane_head((qb, BW))
    lane_head_km = _lane_head((nb, BW))
    blk = lax.broadcasted_iota(jnp.int32, (nb, qb), 0)
    key_pos = lax.broadcasted_iota(jnp.int32, (qb, qb), 0)
    q_pos = lax.broadcasted_iota(jnp.int32, (qb, qb), 1)
    valid = blk < c

    km_heads = jnp.concatenate([jnp.where(lane_head_km == h, km, 0.0) for h in range(HEADS)], axis=0)
    gate_all = _mm_nt(km_heads, q, precision=HIGHEST)

    for h in range(HEADS):
        g = jnp.where(valid, gate_all[h * nb:(h + 1) * nb, :], -jnp.inf)
        sel = jnp.zeros((nb, qb), F32)
        for _ in range(min(MOBA_TOPK, nb)):
            m = jnp.max(g, axis=0, keepdims=True)
            idx = jnp.min(jnp.where(g == m, blk, nb), axis=0, keepdims=True)
            hit = blk == idx
            sel = jnp.where(hit, 1.0, sel)
            g = jnp.where(hit, -jnp.inf, g)
        bias_scr[h] = jnp.where(jnp.logical_and(sel > 0.5, valid), 0.0, NEG)

        qh = jnp.where(lane_head_q == h, q16, jnp.zeros_like(q16))
        qh_scr[h * qb:(h + 1) * qb, :] = qh
        hs = slice(h * HEAD_DIM, (h + 1) * HEAD_DIM)
        s = jnp.where(key_pos <= q_pos, _mm_nt(kb_ref[c], qh), NEG)
        m = jnp.max(s, axis=0, keepdims=True)
        p = jnp.exp2(s - m)
        m_scr[h] = m
        l_scr[h] = jnp.sum(p, axis=0, keepdims=True)
        acc_scr[hs, :] = _mm(vt_ref[c, hs, :], p.astype(BF16))

    s_a[...] = _mm_nt(kb_ref[0], qh_scr[...])

    def step(j, src, dst):
        dst[...] = _mm_nt(kb_ref[jnp.minimum(j + 1, c - 1)], qh_scr[...])
        for h in range(HEADS):
            hs = slice(h * HEAD_DIM, (h + 1) * HEAD_DIM)
            s = src[:, h * qb:(h + 1) * qb] + bias_scr[h, pl.ds(j, 1), :]
            m = m_scr[h]
            m_new = jnp.maximum(m, jnp.max(s, axis=0, keepdims=True))
            a = jnp.exp2(m - m_new)
            p = jnp.exp2(s - m_new)
            m_scr[h] = m_new
            l_scr[h] = a * l_scr[h] + jnp.sum(p, axis=0, keepdims=True)
            acc_scr[hs, :] = a * acc_scr[hs, :] + _mm(vt_ref[j, hs, :], p.astype(BF16))

    def pair(i, carry):
        step(2 * i, s_a, s_b)
        step(2 * i + 1, s_b, s_a)
        return carry

    lax.fori_loop(0, lax.shift_right_logical(c, 1), pair, 0)

    @pl.when(lax.rem(c, 2) == 1)
    def _():
        step(c - 1, s_a, s_b)

    for h in range(HEADS):
        hs = slice(h * HEAD_DIM, (h + 1) * HEAD_DIM)
        acc_scr[hs, :] = acc_scr[hs, :] / l_scr[h]
    yc_ref[...] = acc_scr[...].T.astype(BF16)


def _attn_prompt(q, kb, vt, km):
    b, t, _ = q.shape
    nb = t // MOBA_BLOCK
    return pl.pallas_call(
        functools.partial(_attn_prompt_kernel, nb=nb),
        grid=(b, nb),
        in_specs=[
            pl.BlockSpec((None, MOBA_BLOCK, BW), lambda i, j: (i, j, 0)),
            pl.BlockSpec((None, nb, MOBA_BLOCK, BW), lambda i, j: (i, 0, 0, 0)),
            pl.BlockSpec((None, nb, BW, MOBA_BLOCK), lambda i, j: (i, 0, 0, 0)),
            pl.BlockSpec((None, nb, BW), lambda i, j: (i, 0, 0)),
        ],
        out_specs=pl.BlockSpec((None, MOBA_BLOCK, BW), lambda i, j: (i, j, 0)),
        out_shape=jax.ShapeDtypeStruct((b, t, BW), BF16),
        scratch_shapes=[pltpu.VMEM((HEADS, nb, MOBA_BLOCK), F32), pltpu.VMEM((HEADS * MOBA_BLOCK, BW), BF16),
                        pltpu.VMEM((HEADS, 1, MOBA_BLOCK), F32), pltpu.VMEM((HEADS, 1, MOBA_BLOCK), F32),
                        pltpu.VMEM((BW, MOBA_BLOCK), F32), pltpu.VMEM((MOBA_BLOCK, HEADS * MOBA_BLOCK), F32),
                        pltpu.VMEM((MOBA_BLOCK, HEADS * MOBA_BLOCK), F32)],
        compiler_params=_params("arbitrary", "arbitrary"), name="attn_prompt",
    )(q, kb, vt, km)


def _attn_sample_kernel(pt_ref, ck_ref, cv_ref, q_ref, kn_ref, vn_ref, o_ref, kbuf, vbuf, bsum, ksem, vsem,
                        *, layer, n_pages, ppb):
    b = pl.program_id(0)
    ns = pl.num_programs(0)
    slot = lax.rem(b, 2)
    nbp = n_pages // ppb
    per_head = MOBA_TOPK * ppb

    def k_copy(sample, i, sl):
        return pltpu.make_async_copy(ck_ref.at[layer, pt_ref[sample * n_pages + i]], kbuf.at[sl, i], ksem.at[sl])

    @pl.when(b == 0)
    def _():
        for i in range(n_pages):
            k_copy(0, i, 0).start()

    for i in range(n_pages):
        k_copy(b, i, slot).wait()

    @pl.when(b + 1 < ns)
    def _():
        for i in range(n_pages):
            k_copy(b + 1, i, 1 - slot).start()

    q = q_ref[...]
    group = SUBLANES

    def block_sums(jg, carry):
        rows = [[] for _ in range(HEADS)]
        for jj in range(group):
            tot = None
            for u in range(ppb):
                pr = kbuf[slot, (jg * group + jj) * ppb + u] * q
                tot = pr if tot is None else tot + pr
            for h in range(HEADS):
                rows[h].append(jnp.sum(tot[h * HEAD_DIM:(h + 1) * HEAD_DIM, :], axis=0, keepdims=True))
        for h in range(HEADS):
            bsum[h, pl.ds(pl.multiple_of(jg * group, group), group), :] = jnp.concatenate(rows[h], axis=0)
        return carry

    lax.fori_loop(0, nbp // group, block_sums, 0)

    blk = lax.broadcasted_iota(jnp.int32, (nbp, 1), 0)
    sel = []
    for h in range(HEADS):
        g = jnp.sum(bsum[h], axis=1, keepdims=True) * (1.0 / MOBA_BLOCK)
        for _ in range(MOBA_TOPK):
            idx = jnp.min(jnp.where(g == jnp.max(g), blk, nbp))
            sel.append(idx)
            g = jnp.where(blk == idx, -jnp.inf, g)

    def v_copy(h, r, u):
        n = h * per_head + r * ppb + u
        page = pt_ref[b * n_pages + sel[h * MOBA_TOPK + r] * ppb + u]
        return pltpu.make_async_copy(cv_ref.at[layer, page, pl.ds(h * HEAD_DIM, HEAD_DIM)], vbuf.at[n], vsem.at[0])

    v_copies = [v_copy(h, r, u) for h in range(HEADS) for r in range(MOBA_TOPK) for u in range(ppb)]
    for cp in v_copies:
        cp.start()

    kn = kn_ref[...]
    vn = vn_ref[...]
    probs = []
    for h in range(HEADS):
        hs = slice(h * HEAD_DIM, (h + 1) * HEAD_DIM)
        qh = q[hs, :]
        s_new = jnp.sum(qh * kn[hs, :], axis=0, keepdims=True)
        ss = []
        for r in range(MOBA_TOPK):
            for u in range(ppb):
                kp = kbuf[slot, sel[h * MOBA_TOPK + r] * ppb + u, hs, :]
                ss.append(jnp.sum(kp * qh, axis=0, keepdims=True))
        m = s_new
        for s in ss:
            m = jnp.maximum(m, jnp.max(s, axis=1, keepdims=True))
        p_new = jnp.exp(s_new - m)
        ps = [jnp.exp(s - m) for s in ss]
        l = p_new
        for p in ps:
            l = l + jnp.sum(p, axis=1, keepdims=True)
        probs.append((p_new, ps, l))

    for cp in v_copies:
        cp.wait()

    for h in range(HEADS):
        hs = slice(h * HEAD_DIM, (h + 1) * HEAD_DIM)
        p_new, ps, l = probs[h]
        o = p_new * vn[hs, :]
        for n, p in enumerate(ps):
            o = o + jnp.sum(vbuf[h * per_head + n] * p, axis=1, keepdims=True)
        o_ref[hs, :] = o / l


def _attn_sample(pt_flat, cache_k, cache_v, q, kn, vn, *, layer, n_pages, ppb):
    s = q.shape[0]
    page = cache_k.shape[3]
    n_sel = HEADS * MOBA_TOPK * ppb
    nbp = n_pages // ppb
    assert nbp % SUBLANES == 0
    col = lambda: pl.BlockSpec((None, BW, 1), lambda i, pt: (i, 0, 0))
    grid_spec = pltpu.PrefetchScalarGridSpec(
        num_scalar_prefetch=1, grid=(s,),
        in_specs=[pl.BlockSpec(memory_space=pl.ANY), pl.BlockSpec(memory_space=pl.ANY), col(), col(), col()],
        out_specs=col(),
        scratch_shapes=[pltpu.VMEM((2, n_pages, BW, page), F32), pltpu.VMEM((n_sel, HEAD_DIM, page), F32),
                        pltpu.VMEM((HEADS, nbp, page), F32),
                        pltpu.SemaphoreType.DMA((2,)), pltpu.SemaphoreType.DMA((1,))],
    )
    return pl.pallas_call(
        functools.partial(_attn_sample_kernel, layer=layer, n_pages=n_pages, ppb=ppb),
        grid_spec=grid_spec, out_shape=jax.ShapeDtypeStruct((s, BW, 1), F32),
        compiler_params=_params("arbitrary"), name="attn_sample",
    )(pt_flat, cache_k, cache_v, q[:, :, None], kn[:, :, None], vn[:, :, None])


def _merge_kernel(x_ref, ya_ref, yb_ref, yc_ref, yd_ref, wg_ref, bg_ref, wbr_ref, wo_ref, g_ref, b_ref, o_ref,
                  *, alpha):
    x = x_ref[...]
    xw = x.astype(wg_ref.dtype)
    d = x.shape[1]
    acc = None
    for j, y_ref in enumerate((ya_ref, yb_ref, yc_ref, yd_ref)):
        gate = _sigmoid(_mm(xw, wg_ref[:, j * d:(j + 1) * d]) + bg_ref[:, j * d:(j + 1) * d])
        term = gate * _mm(y_ref[...], wbr_ref[j])
        acc = term if acc is None else acc + term
    out = _mm(acc, wo_ref[...])
    o_ref[...] = _ln(alpha * x + out, g_ref[...], b_ref[...])


def _merge(x, ya, yb, yc, yd, wg, bg, wbr, wo, g, b, *, alpha):
    m, d = x.shape
    tm = _row_tile(m, 256)
    full = lambda a: pl.BlockSpec(a.shape, lambda i: (0,) * a.ndim, pipeline_mode=pl.Buffered(1))
    rows = lambda w: pl.BlockSpec((tm, w), lambda i: (i, 0))
    return pl.pallas_call(
        functools.partial(_merge_kernel, alpha=alpha),
        grid=(m // tm,),
        in_specs=[rows(d), rows(BW), rows(BW), rows(BW), rows(BW), full(wg), full(bg), full(wbr), full(wo),
                  full(g), full(b)],
        out_specs=rows(d), out_shape=jax.ShapeDtypeStruct((m, d), F32),
        compiler_params=_params("arbitrary"), name="merge",
    )(x, ya, yb, yc, yd, wg, bg, wbr, wo, g, b)


def _ffn_kernel(x_ref, wg_ref, wu_ref, wd_ref, g_ref, b_ref, o_ref, acc_ref, *, alpha):
    f = pl.program_id(1)

    @pl.when(f == 0)
    def _():
        acc_ref[...] = jnp.zeros_like(acc_ref)

    xw = x_ref[...].astype(wg_ref.dtype)
    h = _silu(_mm(xw, wg_ref[...])) * _mm(xw, wu_ref[...])
    acc_ref[...] += _mm(h, wd_ref[...])

    @pl.when(f == pl.num_programs(1) - 1)
    def _():
        o_ref[...] = _ln(alpha * x_ref[...] + acc_ref[...], g_ref[...], b_ref[...])


def _ffn_tile(dff):
    return next(c for c in (1408, 1024, 896, 768, 512, 384, 256, 128) if dff % c == 0)


def _ffn(x, wg, wu, wd, g, b, *, alpha):
    m, d = x.shape
    dff = wg.shape[1]
    tm = _row_tile(m, 512)
    tf = _ffn_tile(dff)
    full = lambda a: pl.BlockSpec(a.shape, lambda i, f: (0,) * a.ndim)
    return pl.pallas_call(
        functools.partial(_ffn_kernel, alpha=alpha),
        grid=(m // tm, dff // tf),
        in_specs=[pl.BlockSpec((tm, d), lambda i, f: (i, 0)), pl.BlockSpec((d, tf), lambda i, f: (0, f)),
                  pl.BlockSpec((d, tf), lambda i, f: (0, f)), pl.BlockSpec((tf, d), lambda i, f: (f, 0)),
                  full(g), full(b)],
        out_specs=pl.BlockSpec((tm, d), lambda i, f: (i, 0)), out_shape=jax.ShapeDtypeStruct((m, d), F32),
        scratch_shapes=[pltpu.VMEM((tm, d), F32)],
        compiler_params=_params("arbitrary", "arbitrary"), name="ffn",
    )(x, wg, wu, wd, g, b)


def _router_kernel(x_ref, wr_ref, br_ref, o_ref):
    lg = jnp.dot(x_ref[...], wr_ref[...], precision=HIGHEST, preferred_element_type=F32) + br_ref[...]
    o_ref[...] = _top2(lg)[0]


def _router(x, wr, br):
    m, d = x.shape
    tm = _row_tile(m, 512)
    full = lambda a: pl.BlockSpec(a.shape, lambda i: (0,) * a.ndim)
    return pl.pallas_call(
        _router_kernel, grid=(m // tm,),
        in_specs=[pl.BlockSpec((tm, d), lambda i: (i, 0)), full(wr), full(br)],
        out_specs=pl.BlockSpec((tm, LANES), lambda i: (i, 0)), out_shape=jax.ShapeDtypeStruct((m, LANES), F32),
        compiler_params=_params("arbitrary"), name="router",
    )(x, wr, br)


def _moe_kernel(x_ref, c_ref, wg_ref, wu_ref, wd_ref, g_ref, b_ref, o_ref, acc_ref, *, alpha):
    e = pl.program_id(1)
    f = pl.program_id(2)

    @pl.when(jnp.logical_and(e == 0, f == 0))
    def _():
        acc_ref[...] = jnp.zeros_like(acc_ref)

    comb = c_ref[...]
    lane = lax.broadcasted_iota(jnp.int32, comb.shape, 1)
    ce = jnp.sum(jnp.where(lane == e, comb, 0.0), axis=1, keepdims=True)
    xw = x_ref[...].astype(wg_ref.dtype)
    h = _silu(_mm(xw, wg_ref[...])) * _mm(xw, wu_ref[...])
    acc_ref[...] += ce * _mm(h, wd_ref[...])

    @pl.when(jnp.logical_and(e == pl.num_programs(1) - 1, f == pl.num_programs(2) - 1))
    def _():
        o_ref[...] = _ln(alpha * x_ref[...] + acc_ref[...], g_ref[...], b_ref[...])


def _moe(x, comb, wg, wu, wd, g, b, *, alpha):
    m, d = x.shape
    ne, _, dff = wg.shape
    tm = _row_tile(m, 512)
    tf = _ffn_tile(dff)
    full = lambda a: pl.BlockSpec(a.shape, lambda i, e, f: (0,) * a.ndim)
    return pl.pallas_call(
        functools.partial(_moe_kernel, alpha=alpha),
        grid=(m // tm, ne, dff // tf),
        in_specs=[pl.BlockSpec((tm, d), lambda i, e, f: (i, 0)), pl.BlockSpec((tm, LANES), lambda i, e, f: (i, 0)),
                  pl.BlockSpec((None, d, tf), lambda i, e, f: (e, 0, f)),
                  pl.BlockSpec((None, d, tf), lambda i, e, f: (e, 0, f)),
                  pl.BlockSpec((None, tf, d), lambda i, e, f: (e, f, 0)), full(g), full(b)],
        out_specs=pl.BlockSpec((tm, d), lambda i, e, f: (i, 0)), out_shape=jax.ShapeDtypeStruct((m, d), F32),
        scratch_shapes=[pltpu.VMEM((tm, d), F32)],
        compiler_params=_params("arbitrary", "arbitrary", "arbitrary"), name="moe",
    )(x, comb, wg, wu, wd, g, b)


MOE_TILE = 512
MOE_SUB = 128
FLAG_VALID, FLAG_FIRST, FLAG_LAST = 1, 2, 4


def _top2(lg):
    lane = lax.broadcasted_iota(jnp.int32, lg.shape, 1)
    m1 = jnp.max(lg, axis=1, keepdims=True)
    hit1 = lane == jnp.min(jnp.where(lg == m1, lane, LANES), axis=1, keepdims=True)
    lg2 = jnp.where(hit1, -jnp.inf, lg)
    m2 = jnp.max(lg2, axis=1, keepdims=True)
    hit2 = lane == jnp.min(jnp.where(lg2 == m2, lane, LANES), axis=1, keepdims=True)
    e = jnp.exp(m2 - m1)
    comb = jnp.where(hit1, 1.0 / (1.0 + e), 0.0) + jnp.where(hit2, e / (1.0 + e), 0.0)
    return comb, jnp.logical_or(hit1, hit2)


def _route_kernel(x_ref, wr_ref, br_ref, comb_ref, rank_ref, rankt_ref, cum_ref, carry):
    @pl.when(pl.program_id(0) == 0)
    def _():
        carry[...] = jnp.zeros_like(carry)

    lg = jnp.dot(x_ref[...], wr_ref[...], precision=HIGHEST, preferred_element_type=F32) + br_ref[...]
    comb, chosen = _top2(lg)
    comb_ref[...] = comb
    sel = jnp.where(chosen, 1.0, 0.0)
    tc = sel.shape[0]
    before = lax.broadcasted_iota(jnp.int32, (tc, tc), 0) > lax.broadcasted_iota(jnp.int32, (tc, tc), 1)
    prefix = _mm(jnp.where(before, 1.0, 0.0).astype(BF16), sel.astype(BF16))
    rank = jnp.where(chosen, prefix + carry[0:1, :], -1.0)
    rank_ref[...] = rank
    rankt_ref[...] = rank.T[0:SUBLANES, :]
    carry[...] = carry[...] + jnp.sum(sel, axis=0, keepdims=True)
    cum_ref[...] = carry[...]


def _route(x, wr, br):
    m, d = x.shape
    tc = MOE_TILE
    assert m % tc == 0 and N_EXPERTS <= SUBLANES
    n_ck = m // tc
    full = lambda a: pl.BlockSpec(a.shape, lambda i: (0,) * a.ndim)
    return pl.pallas_call(
        _route_kernel, grid=(n_ck,),
        in_specs=[pl.BlockSpec((tc, d), lambda i: (i, 0)), full(wr), full(br)],
        out_specs=(pl.BlockSpec((tc, LANES), lambda i: (i, 0)), pl.BlockSpec((tc, LANES), lambda i: (i, 0)),
                   pl.BlockSpec((SUBLANES, tc), lambda i: (0, i)),
                   pl.BlockSpec((None, SUBLANES, LANES), lambda i: (i, 0, 0))),
        out_shape=(jax.ShapeDtypeStruct((m, LANES), F32), jax.ShapeDtypeStruct((m, LANES), F32),
                   jax.ShapeDtypeStruct((SUBLANES, m), F32), jax.ShapeDtypeStruct((n_ck, SUBLANES, LANES), F32)),
        scratch_shapes=[pltpu.VMEM((SUBLANES, LANES), F32)],
        compiler_params=_params("arbitrary"), name="route",
    )(x, wr, br)


def _moe_plan(cum_incl, *, n_ck, nt, ni):
    ne, tf = N_EXPERTS, MOE_TILE
    i32 = jnp.int32
    ci = cum_incl[:, 0, :ne].astype(i32)
    ce = jnp.concatenate([jnp.zeros((1, ne), i32), ci[:-1]], axis=0)
    cnt = ci - ce
    gsz = ((ci[-1] + tf - 1) // tf) * tf
    gend = jnp.cumsum(gsz)
    goff = gend - gsz
    tiles = jnp.arange(nt, dtype=i32) * tf
    te = jnp.minimum(jnp.searchsorted(gend, tiles, side="right"), ne - 1).astype(i32)
    tv = (tiles < gend[-1]).astype(i32)

    a = (goff[None, :] + ce).T.reshape(-1)
    c = cnt.T.reshape(-1)
    t_lo = a // tf
    nrep = jnp.where(c > 0, (a + c - 1) // tf - t_lo + 1, 0)
    n_items = jnp.sum(nrep)
    start = jnp.cumsum(nrep) - nrep
    pair = jnp.repeat(jnp.arange(ne * n_ck, dtype=i32), nrep, total_repeat_length=ni)
    w = jnp.arange(ni, dtype=i32)
    valid = w < n_items
    it = t_lo[pair] + (w - start[pair])
    ick = pair % n_ck
    ie = pair // n_ck
    sub_lo = (jnp.maximum(a[pair], it * tf) - it * tf) // MOE_SUB
    sub_hi = (jnp.minimum(a[pair] + c[pair], (it + 1) * tf) - 1 - it * tf) // MOE_SUB
    sub_lo, sub_hi = (jnp.where(valid, v, 0).astype(i32) for v in (sub_lo, sub_hi))

    def finish(it, ick, ie, valid, by_tile):
        last = n_items - 1
        spare = gend[-1] // tf + (w - n_items)
        fill = by_tile & ~valid & (spare < nt)
        it = jnp.where(valid, it, jnp.minimum(spare, nt - 1) if by_tile else it[last])
        ick, ie = (jnp.where(valid, v, v[last]) for v in (ick, ie))
        g = it if by_tile else ick
        first = valid & (g != jnp.concatenate([jnp.full((1,), -1, i32), g[:-1]]))
        final = valid & ((g != jnp.concatenate([g[1:], jnp.full((1,), -1, i32)])) | (w == last))
        flags = valid * FLAG_VALID + (first | fill) * FLAG_FIRST + (final | fill) * FLAG_LAST
        return it.astype(i32), ick.astype(i32), ie.astype(i32), flags.astype(i32)

    by_tile = finish(it, ick, ie, valid, True)
    order = jnp.argsort(jnp.where(valid, ick * nt + it, jnp.iinfo(i32).max))
    by_chunk = finish(it[order], ick[order], ie[order], valid[order], False)
    return goff.astype(i32), te, tv, by_tile + (sub_lo, sub_hi), by_chunk


def _moe_gather_kernel(t_ref, ck_ref, e_ref, flag_ref, goff_ref, lo_ref, hi_ref, x_ref, rankt_ref, xs_ref, acc_ref):
    w = pl.program_id(0)
    flag = flag_ref[w]
    tf, tc = acc_ref.shape[0], x_ref.shape[0]

    @pl.when((flag & FLAG_FIRST) != 0)
    def _():
        acc_ref[...] = jnp.zeros_like(acc_ref)

    @pl.when((flag & FLAG_VALID) != 0)
    def _():
        e = e_ref[w]
        shift = (goff_ref[e] - t_ref[w] * tf).astype(F32)
        dest = rankt_ref[pl.ds(e, 1), :] + shift
        xb = x_ref[...].astype(BF16)

        def sub(sb, carry):
            r0 = pl.multiple_of(sb * MOE_SUB, MOE_SUB)
            row = (lax.broadcasted_iota(jnp.int32, (MOE_SUB, tc), 0) + r0).astype(F32)
            onehot = jnp.where(row == dest, 1.0, 0.0).astype(BF16)
            acc_ref[pl.ds(r0, MOE_SUB), :] += _mm(onehot, xb)
            return carry

        lax.fori_loop(lo_ref[w], hi_ref[w] + 1, sub, 0)

    @pl.when((flag & FLAG_LAST) != 0)
    def _():
        xs_ref[...] = acc_ref[...].astype(BF16)


def _moe_gather(plan, goff, x, rankt, *, nt):
    m, d = x.shape
    tf = tc = MOE_TILE
    t, ck, e, flags, sub_lo, sub_hi = plan
    grid_spec = pltpu.PrefetchScalarGridSpec(
        num_scalar_prefetch=7, grid=(t.shape[0],),
        in_specs=[pl.BlockSpec((tc, d), lambda w, t, ck, *_: (ck[w], 0)),
                  pl.BlockSpec((SUBLANES, tc), lambda w, t, ck, *_: (0, ck[w]))],
        out_specs=pl.BlockSpec((tf, d), lambda w, t, *_: (t[w], 0)),
        scratch_shapes=[pltpu.VMEM((tf, d), F32)],
    )
    return pl.pallas_call(
        _moe_gather_kernel, grid_spec=grid_spec, out_shape=jax.ShapeDtypeStruct((nt * tf, d), BF16),
        compiler_params=_params("arbitrary"), name="moe_gather",
    )(t, ck, e, flags, goff, sub_lo, sub_hi, x, rankt)


def _moe_ffn_kernel(te_ref, tv_ref, xs_ref, wg_ref, wu_ref, wd_ref, ys_ref, acc_ref):
    t = pl.program_id(0)
    f = pl.program_id(1)

    @pl.when(f == 0)
    def _():
        acc_ref[...] = jnp.zeros_like(acc_ref)

    @pl.when(tv_ref[t] != 0)
    def _():
        xs = xs_ref[...]
        h = _silu(_mm(xs, wg_ref[...])) * _mm(xs, wu_ref[...])
        acc_ref[...] += _mm(h, wd_ref[...])

    @pl.when(f == pl.num_programs(1) - 1)
    def _():
        ys_ref[...] = acc_ref[...].astype(BF16)


def _moe_ffn(te, tv, xs, wg, wu, wd):
    r, d = xs.shape
    dff = wg.shape[2]
    tf = MOE_TILE
    tff = _ffn_tile(dff)
    grid_spec = pltpu.PrefetchScalarGridSpec(
        num_scalar_prefetch=2, grid=(r // tf, dff // tff),
        in_specs=[pl.BlockSpec((tf, d), lambda t, f, te, tv: (t, 0)),
                  pl.BlockSpec((None, d, tff), lambda t, f, te, tv: (te[t], 0, f)),
                  pl.BlockSpec((None, d, tff), lambda t, f, te, tv: (te[t], 0, f)),
                  pl.BlockSpec((None, tff, d), lambda t, f, te, tv: (te[t], f, 0))],
        out_specs=pl.BlockSpec((tf, d), lambda t, f, te, tv: (t, 0)),
        scratch_shapes=[pltpu.VMEM((tf, d), F32)],
    )
    return pl.pallas_call(
        _moe_ffn_kernel, grid_spec=grid_spec, out_shape=jax.ShapeDtypeStruct((r, d), BF16),
        compiler_params=_params("arbitrary", "arbitrary"), name="moe_ffn",
    )(te, tv, xs, wg, wu, wd)


def _moe_combine_kernel(t_ref, ck_ref, e_ref, flag_ref, goff_ref, ys_ref, rank_ref, comb_ref, x_ref, g_ref, b_ref,
                        o_ref, acc_ref, *, alpha):
    w = pl.program_id(0)
    flag = flag_ref[w]
    tf, tc = ys_ref.shape[0], x_ref.shape[0]

    @pl.when((flag & FLAG_FIRST) != 0)
    def _():
        acc_ref[...] = jnp.zeros_like(acc_ref)

    @pl.when((flag & FLAG_VALID) != 0)
    def _():
        e = e_ref[w]
        shift = (goff_ref[e] - t_ref[w] * tf).astype(F32)
        mine = lax.broadcasted_iota(jnp.int32, (tc, LANES), 1) == e
        dest = jnp.sum(jnp.where(mine, rank_ref[...], 0.0), axis=1, keepdims=True) + shift
        weight = jnp.sum(jnp.where(mine, comb_ref[...], 0.0), axis=1, keepdims=True)
        col = lax.broadcasted_iota(jnp.int32, (tc, tf), 1).astype(F32)
        onehot = jnp.where(col == dest, 1.0, 0.0).astype(BF16)
        acc_ref[...] += weight * _mm(onehot, ys_ref[...])

    @pl.when((flag & FLAG_LAST) != 0)
    def _():
        o_ref[...] = _ln(alpha * x_ref[...] + acc_ref[...], g_ref[...], b_ref[...])


def _moe_combine(plan, goff, ys, rank, comb, x, g, b, *, alpha):
    m, d = x.shape
    tf = tc = MOE_TILE
    ni = plan[0].shape[0]
    chunk = lambda width: pl.BlockSpec((tc, width), lambda w, t, ck, e, fl, go: (ck[w], 0))
    full = lambda a: pl.BlockSpec(a.shape, lambda w, t, ck, e, fl, go: (0,) * a.ndim)
    grid_spec = pltpu.PrefetchScalarGridSpec(
        num_scalar_prefetch=5, grid=(ni,),
        in_specs=[pl.BlockSpec((tf, d), lambda w, t, ck, e, fl, go: (t[w], 0)), chunk(LANES), chunk(LANES), chunk(d),
                  full(g), full(b)],
        out_specs=chunk(d),
        scratch_shapes=[pltpu.VMEM((tc, d), F32)],
    )
    return pl.pallas_call(
        functools.partial(_moe_combine_kernel, alpha=alpha), grid_spec=grid_spec,
        out_shape=jax.ShapeDtypeStruct((m, d), F32),
        compiler_params=_params("arbitrary"), name="moe_combine",
    )(*plan, goff, ys, rank, comb, x, g, b)


def _moe_routed(x, wr, br, wg, wu, wd, g, b, *, alpha):
    m = x.shape[0]
    tf = MOE_TILE
    n_ck = m // tf
    nt = -(-(2 * m + N_EXPERTS * (tf - 1)) // tf)
    ni = nt + N_EXPERTS * n_ck
    comb, rank, rankt, cum = _route(x, wr, br)
    goff, te, tv, by_tile, by_chunk = _moe_plan(cum, n_ck=n_ck, nt=nt, ni=ni)
    xs = _moe_gather(by_tile, goff, x, rankt, nt=nt)
    ys = _moe_ffn(te, tv, xs, wg, wu, wd)
    return _moe_combine(by_chunk, goff, ys, rank, comb, x, g, b, alpha=alpha)


def _ple_kernel(x_ref, p_ref, wg_ref, wp_ref, g_ref, b_ref, o_ref, *, alpha):
    x = x_ref[...]
    gate = _sigmoid(_mm(x, wg_ref[...]))
    o_ref[...] = _ln(alpha * x + gate * _mm(p_ref[...], wp_ref[...]), g_ref[...], b_ref[...])


def _ple(x, p, wg, wp, g, b, *, alpha):
    m, d = x.shape
    tm = _row_tile(m, 512)
    full = lambda a: pl.BlockSpec(a.shape, lambda i: (0,) * a.ndim)
    return pl.pallas_call(
        functools.partial(_ple_kernel, alpha=alpha),
        grid=(m // tm,),
        in_specs=[pl.BlockSpec((tm, d), lambda i: (i, 0)), pl.BlockSpec((tm, p.shape[1]), lambda i: (i, 0)),
                  full(wg), full(wp), full(g), full(b)],
        out_specs=pl.BlockSpec((tm, d), lambda i: (i, 0)), out_shape=jax.ShapeDtypeStruct((m, d), F32),
        compiler_params=_params("arbitrary"), name="ple",
    )(x, p, wg, wp, g, b)


def kernel(x_prompt, x_sample, p_prompt, p_sample, cache_k, cache_v, state_conv_b, state_conv_d, page_table, w_in, w_gate, b_gate, a_ln_g, a_ln_b, a_w_s, a_b_s, b_conv_w, d_conv_w, d_conv_b, d_ln_g, d_ln_b, w_branch, w_out, ln_g, ln_b, ffn_w_gate, ffn_w_up, ffn_w_down, moe_w_router, moe_b_router, moe_w_gate, moe_w_up, moe_w_down, ple_w_gate, ple_w_proj):
    depth = w_in.shape[0]
    alpha = (2 * depth) ** 0.25
    nbatch, seq, d = x_prompt.shape
    ns, dec_seq, _ = x_sample.shape
    assert dec_seq == 1, "the sample group decodes one token per sequence"
    n_pool, page = cache_k.shape[1], cache_k.shape[2]
    n_pages = page_table.shape[1]
    assert MOBA_BLOCK % page == 0 and (n_pages * page) % MOBA_BLOCK == 0 and (n_pages * page) % CHUNK == 0
    ppb = MOBA_BLOCK // page
    nbp = n_pages // ppb
    assert nbp >= MOBA_TOPK
    pt_flat = page_table.reshape(-1).astype(jnp.int32)
    ck_all = jnp.transpose(cache_k, (0, 1, 3, 4, 2)).reshape(depth, n_pool, BW, page)
    cv_all = jnp.transpose(cache_v, (0, 1, 3, 4, 2)).reshape(depth, n_pool, BW, page)
    row2 = lambda a: a.reshape(1, -1)

    xp = x_prompt
    xs = x_sample.reshape(ns, d)
    outs = {k: [] for k in ("kp", "vp", "ks", "vs", "cbp", "cbs", "cdp", "cds", "cvs")}
    for i in range(depth):
        w_in_i = w_in[i].astype(BF16)
        w_gate_i = w_gate[i].astype(BF16)
        w_branch_i = w_branch[i].astype(BF16)
        w_out_i = w_out[i].astype(BF16)
        alng, alnb, dlng, dlnb, dcb = (row2(a[i]) for a in (a_ln_g, a_ln_b, d_ln_g, d_ln_b, d_conv_b))
        bg = row2(b_gate[i])

        abias = jnp.repeat(a_b_s[i].T, HEAD_DIM, axis=1)
        ya, yb, yd, q, k, v, kb, vt, km, tb, td = _inmix_prompt(
            xp, w_in_i, alng, alnb, a_w_s[i], abias, b_conv_w[i], d_conv_w[i], dcb, dlng, dlnb)
        yc = _attn_prompt(q, kb, vt, km.reshape(nbatch, seq // MOBA_BLOCK, BW))
        outs["kp"].append(k.reshape(nbatch, seq, HEADS, HEAD_DIM))
        outs["vp"].append(v.reshape(nbatch, seq, HEADS, HEAD_DIM))
        outs["cbp"].append(tb)
        outs["cdp"].append(td)
        m = nbatch * seq
        x1p = _merge(xp.reshape(m, d), ya.reshape(m, BW), yb.reshape(m, BW), yc.reshape(m, BW), yd.reshape(m, BW),
                     w_gate_i, bg, w_branch_i, w_out_i, row2(ln_g[i, 0]), row2(ln_b[i, 0]), alpha=alpha)

        w00 = row2(jnp.repeat(a_w_s[i][:, 0, 0], HEAD_DIM))
        b0 = row2(jnp.repeat(a_b_s[i][:, 0], HEAD_DIM))
        hb = jnp.transpose(state_conv_b[i], (1, 0, 2))
        hd = jnp.transpose(state_conv_d[i], (1, 0, 2))
        sya, syb, syd, sq, sk, sv, svn, scbn, scdn = _inmix_sample(
            xs, w_in[i], alng, alnb, w00, b0, b_conv_w[i], hb, d_conv_w[i], dcb, hd, dlng, dlnb)
        syc = _attn_sample(pt_flat, ck_all, cv_all, sq, sk, sv, layer=i, n_pages=n_pages, ppb=ppb)
        outs["ks"].append(sk.reshape(ns, 1, HEADS, HEAD_DIM))
        outs["vs"].append(sv.reshape(ns, 1, HEADS, HEAD_DIM))
        outs["cbs"].append(jnp.concatenate([state_conv_b[i][:, 1:], scbn[:, None]], axis=1))
        outs["cds"].append(jnp.concatenate([state_conv_d[i][:, 1:], scdn[:, None]], axis=1))
        outs["cvs"].append(svn.reshape(ns, 1, BW))
        x1s = _merge(xs, sya, syb, syc.reshape(ns, BW), syd,
                     w_gate[i], bg, w_branch[i], w_out[i], row2(ln_g[i, 0]), row2(ln_b[i, 0]), alpha=alpha)

        g1, b1 = row2(ln_g[i, 1]), row2(ln_b[i, 1])
        j = i // 2
        if i % 2 == 0:
            wg, wu, wd = (w[j].astype(BF16) for w in (ffn_w_gate, ffn_w_up, ffn_w_down))
            x2p = _ffn(x1p, wg, wu, wd, g1, b1, alpha=alpha)
            x2s = _ffn(x1s, ffn_w_gate[j], ffn_w_up[j], ffn_w_down[j], g1, b1, alpha=alpha)
        else:
            wg, wu, wd = (w[j].astype(BF16) for w in (moe_w_gate, moe_w_up, moe_w_down))
            ne = moe_w_router.shape[2]
            wr = jnp.pad(moe_w_router[j], ((0, 0), (0, LANES - ne)))
            br = row2(jnp.pad(moe_b_router[j], (0, LANES - ne), constant_values=NEG))
            x2p = _moe_routed(x1p, wr, br, wg, wu, wd, g1, b1, alpha=alpha)
            x2s = _moe(x1s, _router(x1s, wr, br), wg, wu, wd, g1, b1, alpha=alpha)

        wpg = ple_w_gate[i].astype(BF16)
        wpp = ple_w_proj[i].astype(BF16)
        g2, b2 = row2(ln_g[i, 2]), row2(ln_b[i, 2])
        xp = _ple(x2p, p_prompt[i].reshape(m, -1), wpg, wpp, g2, b2, alpha=alpha).reshape(nbatch, seq, d)
        xs = _ple(x2s, p_sample[i].reshape(ns, -1), ple_w_gate[i], ple_w_proj[i], g2, b2, alpha=alpha)

    st = lambda key: jnp.stack(outs[key])
    return (xp, xs.reshape(ns, 1, d), st("kp"), st("vp"), st("ks"), st("vs"), st("cbp"), st("cbs"), st("cdp"),
            st("cds"), st("cvs"))
```

```python
import functools
import math

import jax
import jax.numpy as jnp
from jax import lax
from jax.experimental import pallas as pl
from jax.experimental.pallas import tpu as pltpu

F32 = jnp.float32
BF16 = jnp.bfloat16
HIGHEST = lax.Precision.HIGHEST

BW = 256
N_BRANCH = 4
CHUNK = 128
HEADS = 4
HEAD_DIM = BW // HEADS
HEAD_SHIFT = 6
MOBA_BLOCK = 256
MOBA_TOPK = 3
B_CONV = 3
D_CONV = 31
N_EXPERTS = 8
LN_EPS = 1e-5
NEG = -1e30
LOG2E = math.log2(math.e)
LANES = 128
SUBLANES = 8
V7X_VMEM_LIMIT = 56 * 1024 * 1024
CONV_ROWS = 64
B_HALO = 8
D_HALO = 32


def _params(*sem):
    return pltpu.CompilerParams(dimension_semantics=sem, vmem_limit_bytes=V7X_VMEM_LIMIT)


def _ln(x, g, b):
    mu = jnp.mean(x, axis=-1, keepdims=True)
    xc = x - mu
    var = jnp.mean(xc * xc, axis=-1, keepdims=True)
    return xc * lax.rsqrt(var + LN_EPS) * g + b


def _gelu(x):
    c = math.sqrt(2.0 / math.pi)
    return x * (0.5 * (1.0 + jnp.tanh(c * (x + 0.044715 * (x * x * x)))))


def _sigmoid(x):
    return 1.0 / (1.0 + jnp.exp(-x))


def _silu(x):
    return x * _sigmoid(x)


def _mm(a, b):
    if b.dtype == F32:
        return jnp.dot(a.astype(F32), b, precision=HIGHEST, preferred_element_type=F32)
    return jnp.dot(a.astype(BF16), b, preferred_element_type=F32)


def _mm_nt(a, b, **kw):
    return lax.dot_general(a, b, (((1,), (1,)), ((), ())), preferred_element_type=F32, **kw)


def _lane_head(shape):
    return lax.shift_right_logical(lax.broadcasted_iota(jnp.int32, shape, len(shape) - 1), HEAD_SHIFT)


def _row_tile(m, want):
    tm = min(want, m)
    assert m % tm == 0, (m, tm)
    return tm


def _inmix_prompt_kernel(x_ref, w_ref, alng_ref, alnb_ref, aws_ref, abias_ref, bcw_ref, dcw_ref, dcb_ref,
                         dlng_ref, dlnb_ref,
                         ya_ref, yb_ref, yd_ref, q_ref, k_ref, v_ref, kb_ref, vt_ref, km_ref, tb_ref, td_ref,
                         cb_buf, cd_buf, *, tm, scale):
    t = pl.program_id(1)
    last = pl.num_programs(1) - 1

    @pl.when(t == 0)
    def _():
        cb_buf[0:B_HALO, :] = jnp.zeros((B_HALO, BW), F32)
        cd_buf[0:D_HALO, :] = jnp.zeros((D_HALO, BW), F32)
        cd_buf[D_HALO + tm:D_HALO + tm + SUBLANES, :] = jnp.zeros((SUBLANES, BW), F32)

    xb = x_ref[...].astype(BF16)

    def proj(j):
        return _mm(xb, w_ref[:, j * BW:(j + 1) * BW])

    gu = _gelu(proj(0))
    vnb = _ln(_gelu(proj(1)), alng_ref[...], alnb_ref[...]).astype(BF16)
    row = lax.broadcasted_iota(jnp.int32, (CHUNK, CHUNK), 0)
    col = lax.broadcasted_iota(jnp.int32, (CHUNK, CHUNK), 1)
    wts = [jnp.where(row >= col, aws_ref[h], 0.0).astype(BF16) for h in range(HEADS)]
    lane_head = _lane_head((CHUNK, BW))
    for c in range(tm // CHUNK):
        sl = slice(c * CHUNK, (c + 1) * CHUNK)
        vc = vnb[sl, :]
        s = abias_ref[...]
        for h in range(HEADS):
            s = s + _mm(wts[h], jnp.where(lane_head == h, vc, jnp.zeros_like(vc)))
        ya_ref[sl, :] = (gu[sl, :] * s).astype(BF16)

    bb = proj(2)
    cb_buf[B_HALO:B_HALO + tm, :] = proj(3) * proj(4)
    off_b = B_HALO - (B_CONV - 1)
    for r0 in range(0, tm, CONV_ROWS):
        acc = jnp.zeros((CONV_ROWS, BW), F32)
        for kk in range(B_CONV):
            acc = acc + bcw_ref[kk:kk + 1, :] * cb_buf[r0 + kk + off_b:r0 + kk + off_b + CONV_ROWS, :]
        yb_ref[r0:r0 + CONV_ROWS, :] = (bb[r0:r0 + CONV_ROWS, :] * acc).astype(BF16)

    @pl.when(t == last)
    def _():
        tb_ref[...] = cb_buf[tm + off_b:tm + B_HALO, :]

    cb_buf[0:B_HALO, :] = cb_buf[tm:tm + B_HALO, :]

    q_ref[...] = proj(5) * scale
    k = proj(6)
    v = proj(7)
    k_ref[...] = k
    v_ref[...] = v
    for i in range(tm // MOBA_BLOCK):
        sl = slice(i * MOBA_BLOCK, (i + 1) * MOBA_BLOCK)
        kblk = k[sl, :]
        kb_ref[i] = kblk.astype(BF16)
        vt_ref[i] = v[sl, :].T.astype(BF16)
        km_ref[i] = jnp.sum(kblk, axis=0, keepdims=True) * (1.0 / MOBA_BLOCK)

    cd_buf[D_HALO:D_HALO + tm, :] = proj(8) * _sigmoid(proj(9))
    off_d = D_HALO - (D_CONV - 1)
    span = CONV_ROWS + D_HALO + SUBLANES
    for r0 in range(0, tm, CONV_ROWS):
        acc = jnp.zeros((CONV_ROWS, BW), F32) + dcb_ref[...]
        block = cd_buf[r0:r0 + span, :]
        for r in range(SUBLANES):
            win = block if r == 0 else pltpu.roll(block, span - r, 0)
            for off in range(r, D_HALO + 1, SUBLANES):
                kk = off - off_d
                if 0 <= kk < D_CONV:
                    acc = acc + dcw_ref[kk:kk + 1, :] * win[off - r:off - r + CONV_ROWS, :]
        yd_ref[r0:r0 + CONV_ROWS, :] = _silu(_ln(acc, dlng_ref[...], dlnb_ref[...])).astype(BF16)

    @pl.when(t == last)
    def _():
        td_ref[...] = cd_buf[tm + off_d:tm + D_HALO, :]

    cd_buf[0:D_HALO, :] = cd_buf[tm:tm + D_HALO, :]


def _inmix_prompt(x, w_in, alng, alnb, aws, abias, bcw, dcw, dcb, dlng, dlnb):
    b, t, d = x.shape
    tm = _row_tile(t, 512)
    assert tm % MOBA_BLOCK == 0 and tm % CONV_ROWS == 0
    nbt = tm // MOBA_BLOCK
    nb = t // MOBA_BLOCK
    full = lambda shape: pl.BlockSpec(shape, lambda i, j: (0,) * len(shape))
    rows = lambda: pl.BlockSpec((None, tm, BW), lambda i, j: (i, j, 0))
    blocks = lambda: pl.BlockSpec((None, nbt, MOBA_BLOCK, MOBA_BLOCK), lambda i, j: (i, j, 0, 0))
    out_shape = (
        jax.ShapeDtypeStruct((b, t, BW), BF16),
        jax.ShapeDtypeStruct((b, t, BW), BF16),
        jax.ShapeDtypeStruct((b, t, BW), BF16),
        jax.ShapeDtypeStruct((b, t, BW), F32),
        jax.ShapeDtypeStruct((b, t, BW), F32),
        jax.ShapeDtypeStruct((b, t, BW), F32),
        jax.ShapeDtypeStruct((b, nb, MOBA_BLOCK, BW), BF16),
        jax.ShapeDtypeStruct((b, nb, BW, MOBA_BLOCK), BF16),
        jax.ShapeDtypeStruct((b, nb, 1, BW), F32),
        jax.ShapeDtypeStruct((b, B_CONV - 1, BW), F32),
        jax.ShapeDtypeStruct((b, D_CONV - 1, BW), F32),
    )
    out_specs = (
        rows(), rows(), rows(), rows(), rows(), rows(), blocks(), blocks(),
        pl.BlockSpec((None, nbt, 1, BW), lambda i, j: (i, j, 0, 0)),
        pl.BlockSpec((None, B_CONV - 1, BW), lambda i, j: (i, 0, 0)),
        pl.BlockSpec((None, D_CONV - 1, BW), lambda i, j: (i, 0, 0)),
    )
    in_specs = [
        pl.BlockSpec((None, tm, d), lambda i, j: (i, j, 0)),
        full(w_in.shape), full(alng.shape), full(alnb.shape), full(aws.shape), full(abias.shape),
        full(bcw.shape), full(dcw.shape), full(dcb.shape), full(dlng.shape), full(dlnb.shape),
    ]
    return pl.pallas_call(
        functools.partial(_inmix_prompt_kernel, tm=tm, scale=LOG2E / math.sqrt(HEAD_DIM)),
        grid=(b, t // tm), in_specs=in_specs, out_specs=out_specs, out_shape=out_shape,
        scratch_shapes=[pltpu.VMEM((B_HALO + tm, BW), F32), pltpu.VMEM((D_HALO + tm + SUBLANES, BW), F32)],
        compiler_params=_params("arbitrary", "arbitrary"), name="inmix_prompt",
    )(x, w_in, alng, alnb, aws, abias, bcw, dcw, dcb, dlng, dlnb)


def _inmix_sample_kernel(x_ref, w_ref, alng_ref, alnb_ref, w00_ref, b0_ref, bcw_ref, hb_ref, dcw_ref, dcb_ref,
                         hd_ref, dlng_ref, dlnb_ref,
                         ya_ref, yb_ref, yd_ref, q_ref, k_ref, v_ref, vn_ref, cbn_ref, cdn_ref, *, scale):
    x = x_ref[...]

    def proj(j):
        return _mm(x, w_ref[:, j * BW:(j + 1) * BW])

    vn = _ln(_gelu(proj(1)), alng_ref[...], alnb_ref[...])
    vn_ref[...] = vn
    ya_ref[...] = _gelu(proj(0)) * (vn * w00_ref[...] + b0_ref[...])

    cbn = proj(3) * proj(4)
    cbn_ref[...] = cbn
    conv = bcw_ref[B_CONV - 1:B_CONV, :] * cbn
    for kk in range(B_CONV - 1):
        conv = conv + bcw_ref[kk:kk + 1, :] * hb_ref[kk]
    yb_ref[...] = proj(2) * conv

    q_ref[...] = proj(5) * scale
    k_ref[...] = proj(6)
    v_ref[...] = proj(7)

    cdn = proj(8) * _sigmoid(proj(9))
    cdn_ref[...] = cdn
    conv = dcb_ref[...] + dcw_ref[D_CONV - 1:D_CONV, :] * cdn
    for kk in range(D_CONV - 1):
        conv = conv + dcw_ref[kk:kk + 1, :] * hd_ref[kk]
    yd_ref[...] = _silu(_ln(conv, dlng_ref[...], dlnb_ref[...]))


def _inmix_sample(x, w_in, alng, alnb, w00, b0, bcw, hb, dcw, dcb, hd, dlng, dlnb):
    s = x.shape[0]
    args = (x, w_in, alng, alnb, w00, b0, bcw, hb, dcw, dcb, hd, dlng, dlnb)
    full = lambda shape: pl.BlockSpec(shape, lambda i: (0,) * len(shape))
    row = lambda dt: jax.ShapeDtypeStruct((s, BW), dt)
    out_shape = tuple(row(F32) for _ in range(9))
    return pl.pallas_call(
        functools.partial(_inmix_sample_kernel, scale=1.0 / math.sqrt(HEAD_DIM)),
        grid=(1,), in_specs=[full(a.shape) for a in args], out_specs=tuple(full((s, BW)) for _ in out_shape),
        out_shape=out_shape, compiler_params=_params("arbitrary"), name="inmix_sample",
    )(*args)


def _attn_prompt_kernel(q_ref, kb_ref, vt_ref, km_ref, yc_ref, bias_scr, qh_scr, m_scr, l_scr, acc_scr, s_a, s_b,
                        *, nb):
    c = pl.program_id(1)
    qb = MOBA_BLOCK
    q = q_ref[...]
    q16 = q.astype(BF16)
    km = km_ref[...]
    lane_head_q = _lane_head((qb, BW))
    lane_head_km = _lane_head((nb, BW))
    blk = lax.broadcasted_iota(jnp.int32, (nb, qb), 0)
    key_pos = lax.broadcasted_iota(jnp.int32, (qb, qb), 0)
    q_pos = lax.broadcasted_iota(jnp.int32, (qb, qb), 1)
    valid = blk < c

    km_heads = jnp.concatenate([jnp.where(lane_head_km == h, km, 0.0) for h in range(HEADS)], axis=0)
    gate_all = _mm_nt(km_heads, q, precision=HIGHEST)

    for h in range(HEADS):
        g = jnp.where(valid, gate_all[h * nb:(h + 1) * nb, :], -jnp.inf)
        sel = jnp.zeros((nb, qb), F32)
        for _ in range(min(MOBA_TOPK, nb)):
            m = jnp.max(g, axis=0, keepdims=True)
            idx = jnp.min(jnp.where(g == m, blk, nb), axis=0, keepdims=True)
            hit = blk == idx
            sel = jnp.where(hit, 1.0, sel)
            g = jnp.where(hit, -jnp.inf, g)
        bias_scr[h] = jnp.where(jnp.logical_and(sel > 0.5, valid), 0.0, NEG)

        qh = jnp.where(lane_head_q == h, q16, jnp.zeros_like(q16))
        qh_scr[h * qb:(h + 1) * qb, :] = qh
        hs = slice(h * HEAD_DIM, (h + 1) * HEAD_DIM)
        s = jnp.where(key_pos <= q_pos, _mm_nt(kb_ref[c], qh), NEG)
        m = jnp.max(s, axis=0, keepdims=True)
        p = jnp.exp2(s - m)
        m_scr[h] = m
        l_scr[h] = jnp.sum(p, axis=0, keepdims=True)
        acc_scr[hs, :] = _mm(vt_ref[c, hs, :], p.astype(BF16))

    s_a[...] = _mm_nt(kb_ref[0], qh_scr[...])

    def step(j, src, dst):
        dst[...] = _mm_nt(kb_ref[jnp.minimum(j + 1, c - 1)], qh_scr[...])
        for h in range(HEADS):
            hs = slice(h * HEAD_DIM, (h + 1) * HEAD_DIM)
            s = src[:, h * qb:(h + 1) * qb] + bias_scr[h, pl.ds(j, 1), :]
            m = m_scr[h]
            m_new = jnp.maximum(m, jnp.max(s, axis=0, keepdims=True))
            a = jnp.exp2(m - m_new)
            p = jnp.exp2(s - m_new)
            m_scr[h] = m_new
            l_scr[h] = a * l_scr[h] + jnp.sum(p, axis=0, keepdims=True)
            acc_scr[hs, :] = a * acc_scr[hs, :] + _mm(vt_ref[j, hs, :], p.astype(BF16))

    def pair(i, carry):
        step(2 * i, s_a, s_b)
        step(2 * i + 1, s_b, s_a)
        return carry

    lax.fori_loop(0, lax.shift_right_logical(c, 1), pair, 0)

    @pl.when(lax.rem(c, 2) == 1)
    def _():
        step(c - 1, s_a, s_b)

    for h in range(HEADS):
        hs = slice(h * HEAD_DIM, (h + 1) * HEAD_DIM)
        acc_scr[hs, :] = acc_scr[hs, :] / l_scr[h]
    yc_ref[...] = acc_scr[...].T.astype(BF16)


def _attn_prompt(q, kb, vt, km):
    b, t, _ = q.shape
    nb = t // MOBA_BLOCK
    return pl.pallas_call(
        functools.partial(_attn_prompt_kernel, nb=nb),
        grid=(b, nb),
        in_specs=[
            pl.BlockSpec((None, MOBA_BLOCK, BW), lambda i, j: (i, j, 0)),
            pl.BlockSpec((None, nb, MOBA_BLOCK, BW), lambda i, j: (i, 0, 0, 0)),
            pl.BlockSpec((None, nb, BW, MOBA_BLOCK), lambda i, j: (i, 0, 0, 0)),
            pl.BlockSpec((None, nb, BW), lambda i, j: (i, 0, 0)),
        ],
        out_specs=pl.BlockSpec((None, MOBA_BLOCK, BW), lambda i, j: (i, j, 0)),
        out_shape=jax.ShapeDtypeStruct((b, t, BW), BF16),
        scratch_shapes=[pltpu.VMEM((HEADS, nb, MOBA_BLOCK), F32), pltpu.VMEM((HEADS * MOBA_BLOCK, BW), BF16),
                        pltpu.VMEM((HEADS, 1, MOBA_BLOCK), F32), pltpu.VMEM((HEADS, 1, MOBA_BLOCK), F32),
                        pltpu.VMEM((BW, MOBA_BLOCK), F32), pltpu.VMEM((MOBA_BLOCK, HEADS * MOBA_BLOCK), F32),
                        pltpu.VMEM((MOBA_BLOCK, HEADS * MOBA_BLOCK), F32)],
        compiler_params=_params("arbitrary", "arbitrary"), name="attn_prompt",
    )(q, kb, vt, km)


def _attn_sample_kernel(pt_ref, ck_ref, cv_ref, q_ref, kn_ref, vn_ref, o_ref, kbuf, vbuf, bsum, ksem, vsem,
                        *, layer, n_pages, ppb):
    b = pl.program_id(0)
    ns = pl.num_programs(0)
    slot = lax.rem(b, 2)
    nbp = n_pages // ppb
    per_head = MOBA_TOPK * ppb

    def k_copy(sample, i, sl):
        return pltpu.make_async_copy(ck_ref.at[layer, pt_ref[sample * n_pages + i]], kbuf.at[sl, i], ksem.at[sl])

    @pl.when(b == 0)
    def _():
        for i in range(n_pages):
            k_copy(0, i, 0).start()

    for i in range(n_pages):
        k_copy(b, i, slot).wait()

    @pl.when(b + 1 < ns)
    def _():
        for i in range(n_pages):
            k_copy(b + 1, i, 1 - slot).start()

    q = q_ref[...]
    group = SUBLANES

    def block_sums(jg, carry):
        rows = [[] for _ in range(HEADS)]
        for jj in range(group):
            tot = None
            for u in range(ppb):
                pr = kbuf[slot, (jg * group + jj) * ppb + u] * q
                tot = pr if tot is None else tot + pr
            for h in range(HEADS):
                rows[h].append(jnp.sum(tot[h * HEAD_DIM:(h + 1) * HEAD_DIM, :], axis=0, keepdims=True))
        for h in range(HEADS):
            bsum[h, pl.ds(pl.multiple_of(jg * group, group), group), :] = jnp.concatenate(rows[h], axis=0)
        return carry

    lax.fori_loop(0, nbp // group, block_sums, 0)

    blk = lax.broadcasted_iota(jnp.int32, (nbp, 1), 0)
    sel = []
    for h in range(HEADS):
        g = jnp.sum(bsum[h], axis=1, keepdims=True) * (1.0 / MOBA_BLOCK)
        for _ in range(MOBA_TOPK):
            idx = jnp.min(jnp.where(g == jnp.max(g), blk, nbp))
            sel.append(idx)
            g = jnp.where(blk == idx, -jnp.inf, g)

    def v_copy(h, r, u):
        n = h * per_head + r * ppb + u
        page = pt_ref[b * n_pages + sel[h * MOBA_TOPK + r] * ppb + u]
        return pltpu.make_async_copy(cv_ref.at[layer, page, pl.ds(h * HEAD_DIM, HEAD_DIM)], vbuf.at[n], vsem.at[0])

    v_copies = [v_copy(h, r, u) for h in range(HEADS) for r in range(MOBA_TOPK) for u in range(ppb)]
    for cp in v_copies:
        cp.start()

    kn = kn_ref[...]
    vn = vn_ref[...]
    probs = []
    for h in range(HEADS):
        hs = slice(h * HEAD_DIM, (h + 1) * HEAD_DIM)
        qh = q[hs, :]
        s_new = jnp.sum(qh * kn[hs, :], axis=0, keepdims=True)
        ss = []
        for r in range(MOBA_TOPK):
            for u in range(ppb):
                kp = kbuf[slot, sel[h * MOBA_TOPK + r] * ppb + u, hs, :]
                ss.append(jnp.sum(kp * qh, axis=0, keepdims=True))
        m = s_new
        for s in ss:
            m = jnp.maximum(m, jnp.max(s, axis=1, keepdims=True))
        p_new = jnp.exp(s_new - m)
        ps = [jnp.exp(s - m) for s in ss]
        l = p_new
        for p in ps:
            l = l + jnp.sum(p, axis=1, keepdims=True)
        probs.append((p_new, ps, l))

    for cp in v_copies:
        cp.wait()

    for h in range(HEADS):
        hs = slice(h * HEAD_DIM, (h + 1) * HEAD_DIM)
        p_new, ps, l = probs[h]
        o = p_new * vn[hs, :]
        for n, p in enumerate(ps):
            o = o + jnp.sum(vbuf[h * per_head + n] * p, axis=1, keepdims=True)
        o_ref[hs, :] = o / l


def _attn_sample(pt_flat, cache_k, cache_v, q, kn, vn, *, layer, n_pages, ppb):
    s = q.shape[0]
    page = cache_k.shape[3]
    n_sel = HEADS * MOBA_TOPK * ppb
    nbp = n_pages // ppb
    assert nbp % SUBLANES == 0
    col = lambda: pl.BlockSpec((None, BW, 1), lambda i, pt: (i, 0, 0))
    grid_spec = pltpu.PrefetchScalarGridSpec(
        num_scalar_prefetch=1, grid=(s,),
        in_specs=[pl.BlockSpec(memory_space=pl.ANY), pl.BlockSpec(memory_space=pl.ANY), col(), col(), col()],
        out_specs=col(),
        scratch_shapes=[pltpu.VMEM((2, n_pages, BW, page), F32), pltpu.VMEM((n_sel, HEAD_DIM, page), F32),
                        pltpu.VMEM((HEADS, nbp, page), F32),
                        pltpu.SemaphoreType.DMA((2,)), pltpu.SemaphoreType.DMA((1,))],
    )
    return pl.pallas_call(
        functools.partial(_attn_sample_kernel, layer=layer, n_pages=n_pages, ppb=ppb),
        grid_spec=grid_spec, out_shape=jax.ShapeDtypeStruct((s, BW, 1), F32),
        compiler_params=_params("arbitrary"), name="attn_sample",
    )(pt_flat, cache_k, cache_v, q[:, :, None], kn[:, :, None], vn[:, :, None])


def _merge_kernel(x_ref, ya_ref, yb_ref, yc_ref, yd_ref, wg_ref, bg_ref, wbr_ref, wo_ref, g_ref, b_ref, o_ref,
                  o16_ref, *, alpha):
    x = x_ref[...]
    xw = x.astype(wg_ref.dtype)
    d = x.shape[1]
    acc = None
    for j, y_ref in enumerate((ya_ref, yb_ref, yc_ref, yd_ref)):
        gate = _sigmoid(_mm(xw, wg_ref[:, j * d:(j + 1) * d]) + bg_ref[:, j * d:(j + 1) * d])
        term = gate * _mm(y_ref[...], wbr_ref[j])
        acc = term if acc is None else acc + term
    out = _mm(acc, wo_ref[...])
    res = _ln(alpha * x + out, g_ref[...], b_ref[...])
    o_ref[...] = res
    o16_ref[...] = res.astype(BF16)


def _merge(x, ya, yb, yc, yd, wg, bg, wbr, wo, g, b, *, alpha):
    m, d = x.shape
    tm = _row_tile(m, 512)
    full = lambda a: pl.BlockSpec(a.shape, lambda i: (0,) * a.ndim, pipeline_mode=pl.Buffered(1))
    rows = lambda w: pl.BlockSpec((tm, w), lambda i: (i, 0))
    return pl.pallas_call(
        functools.partial(_merge_kernel, alpha=alpha),
        grid=(m // tm,),
        in_specs=[rows(d), rows(BW), rows(BW), rows(BW), rows(BW), full(wg), full(bg), full(wbr), full(wo),
                  full(g), full(b)],
        out_specs=(rows(d), rows(d)),
        out_shape=(jax.ShapeDtypeStruct((m, d), F32), jax.ShapeDtypeStruct((m, d), BF16)),
        compiler_params=_params("arbitrary"), name="merge",
    )(x, ya, yb, yc, yd, wg, bg, wbr, wo, g, b)


def _ffn_kernel(x_ref, wg_ref, wu_ref, wd_ref, g_ref, b_ref, o_ref, acc_ref, *, alpha):
    f = pl.program_id(1)

    @pl.when(f == 0)
    def _():
        acc_ref[...] = jnp.zeros_like(acc_ref)

    xw = x_ref[...].astype(wg_ref.dtype)
    h = _silu(_mm(xw, wg_ref[...])) * _mm(xw, wu_ref[...])
    acc_ref[...] += _mm(h, wd_ref[...])

    @pl.when(f == pl.num_programs(1) - 1)
    def _():
        o_ref[...] = _ln(alpha * x_ref[...] + acc_ref[...], g_ref[...], b_ref[...])


def _ffn_tile(dff):
    return next(c for c in (1408, 1024, 896, 768, 512, 384, 256, 128) if dff % c == 0)


def _ffn(x, wg, wu, wd, g, b, *, alpha):
    m, d = x.shape
    dff = wg.shape[1]
    tm = _row_tile(m, 512)
    tf = _ffn_tile(dff)
    full = lambda a: pl.BlockSpec(a.shape, lambda i, f: (0,) * a.ndim)
    return pl.pallas_call(
        functools.partial(_ffn_kernel, alpha=alpha),
        grid=(m // tm, dff // tf),
        in_specs=[pl.BlockSpec((tm, d), lambda i, f: (i, 0)), pl.BlockSpec((d, tf), lambda i, f: (0, f)),
                  pl.BlockSpec((d, tf), lambda i, f: (0, f)), pl.BlockSpec((tf, d), lambda i, f: (f, 0)),
                  full(g), full(b)],
        out_specs=pl.BlockSpec((tm, d), lambda i, f: (i, 0)), out_shape=jax.ShapeDtypeStruct((m, d), F32),
        scratch_shapes=[pltpu.VMEM((tm, d), F32)],
        compiler_params=_params("arbitrary", "arbitrary"), name="ffn",
    )(x, wg, wu, wd, g, b)


def _router_kernel(x_ref, wr_ref, br_ref, o_ref):
    lg = jnp.dot(x_ref[...], wr_ref[...], precision=HIGHEST, preferred_element_type=F32) + br_ref[...]
    o_ref[...] = _top2(lg)[0]


def _router(x, wr, br):
    m, d = x.shape
    tm = _row_tile(m, 512)
    full = lambda a: pl.BlockSpec(a.shape, lambda i: (0,) * a.ndim)
    return pl.pallas_call(
        _router_kernel, grid=(m // tm,),
        in_specs=[pl.BlockSpec((tm, d), lambda i: (i, 0)), full(wr), full(br)],
        out_specs=pl.BlockSpec((tm, LANES), lambda i: (i, 0)), out_shape=jax.ShapeDtypeStruct((m, LANES), F32),
        compiler_params=_params("arbitrary"), name="router",
    )(x, wr, br)


def _moe_kernel(x_ref, c_ref, wg_ref, wu_ref, wd_ref, g_ref, b_ref, o_ref, acc_ref, *, alpha):
    e = pl.program_id(1)
    f = pl.program_id(2)

    @pl.when(jnp.logical_and(e == 0, f == 0))
    def _():
        acc_ref[...] = jnp.zeros_like(acc_ref)

    comb = c_ref[...]
    lane = lax.broadcasted_iota(jnp.int32, comb.shape, 1)
    ce = jnp.sum(jnp.where(lane == e, comb, 0.0), axis=1, keepdims=True)
    xw = x_ref[...].astype(wg_ref.dtype)
    h = _silu(_mm(xw, wg_ref[...])) * _mm(xw, wu_ref[...])
    acc_ref[...] += ce * _mm(h, wd_ref[...])

    @pl.when(jnp.logical_and(e == pl.num_programs(1) - 1, f == pl.num_programs(2) - 1))
    def _():
        o_ref[...] = _ln(alpha * x_ref[...] + acc_ref[...], g_ref[...], b_ref[...])


def _moe(x, comb, wg, wu, wd, g, b, *, alpha):
    m, d = x.shape
    ne, _, dff = wg.shape
    tm = _row_tile(m, 512)
    tf = _ffn_tile(dff)
    full = lambda a: pl.BlockSpec(a.shape, lambda i, e, f: (0,) * a.ndim)
    return pl.pallas_call(
        functools.partial(_moe_kernel, alpha=alpha),
        grid=(m // tm, ne, dff // tf),
        in_specs=[pl.BlockSpec((tm, d), lambda i, e, f: (i, 0)), pl.BlockSpec((tm, LANES), lambda i, e, f: (i, 0)),
                  pl.BlockSpec((None, d, tf), lambda i, e, f: (e, 0, f)),
                  pl.BlockSpec((None, d, tf), lambda i, e, f: (e, 0, f)),
                  pl.BlockSpec((None, tf, d), lambda i, e, f: (e, f, 0)), full(g), full(b)],
        out_specs=pl.BlockSpec((tm, d), lambda i, e, f: (i, 0)), out_shape=jax.ShapeDtypeStruct((m, d), F32),
        scratch_shapes=[pltpu.VMEM((tm, d), F32)],
        compiler_params=_params("arbitrary", "arbitrary", "arbitrary"), name="moe",
    )(x, comb, wg, wu, wd, g, b)


MOE_TILE = 512
MOE_SUB = 128
FLAG_VALID, FLAG_FIRST, FLAG_LAST = 1, 2, 4


def _top2(lg):
    lane = lax.broadcasted_iota(jnp.int32, lg.shape, 1)
    m1 = jnp.max(lg, axis=1, keepdims=True)
    hit1 = lane == jnp.min(jnp.where(lg == m1, lane, LANES), axis=1, keepdims=True)
    lg2 = jnp.where(hit1, -jnp.inf, lg)
    m2 = jnp.max(lg2, axis=1, keepdims=True)
    hit2 = lane == jnp.min(jnp.where(lg2 == m2, lane, LANES), axis=1, keepdims=True)
    e = jnp.exp(m2 - m1)
    comb = jnp.where(hit1, 1.0 / (1.0 + e), 0.0) + jnp.where(hit2, e / (1.0 + e), 0.0)
    return comb, jnp.logical_or(hit1, hit2)


def _route_kernel(x_ref, wr_ref, br_ref, comb_ref, rank_ref, rankt_ref, cum_ref, carry):
    @pl.when(pl.program_id(0) == 0)
    def _():
        carry[...] = jnp.zeros_like(carry)

    lg = jnp.dot(x_ref[...], wr_ref[...], precision=HIGHEST, preferred_element_type=F32) + br_ref[...]
    comb, chosen = _top2(lg)
    comb_ref[...] = comb
    sel = jnp.where(chosen, 1.0, 0.0)
    tc = sel.shape[0]
    before = lax.broadcasted_iota(jnp.int32, (tc, tc), 0) > lax.broadcasted_iota(jnp.int32, (tc, tc), 1)
    prefix = _mm(jnp.where(before, 1.0, 0.0).astype(BF16), sel.astype(BF16))
    rank = jnp.where(chosen, prefix + carry[0:1, :], -1.0)
    rank_ref[...] = rank
    rankt_ref[...] = rank.T[0:SUBLANES, :]
    carry[...] = carry[...] + jnp.sum(sel, axis=0, keepdims=True)
    cum_ref[...] = carry[...]


def _route(x, wr, br):
    m, d = x.shape
    tc = MOE_TILE
    assert m % tc == 0 and N_EXPERTS <= SUBLANES
    n_ck = m // tc
    full = lambda a: pl.BlockSpec(a.shape, lambda i: (0,) * a.ndim)
    return pl.pallas_call(
        _route_kernel, grid=(n_ck,),
        in_specs=[pl.BlockSpec((tc, d), lambda i: (i, 0)), full(wr), full(br)],
        out_specs=(pl.BlockSpec((tc, LANES), lambda i: (i, 0)), pl.BlockSpec((tc, LANES), lambda i: (i, 0)),
                   pl.BlockSpec((SUBLANES, tc), lambda i: (0, i)),
                   pl.BlockSpec((None, SUBLANES, LANES), lambda i: (i, 0, 0))),
        out_shape=(jax.ShapeDtypeStruct((m, LANES), F32), jax.ShapeDtypeStruct((m, LANES), F32),
                   jax.ShapeDtypeStruct((SUBLANES, m), F32), jax.ShapeDtypeStruct((n_ck, SUBLANES, LANES), F32)),
        scratch_shapes=[pltpu.VMEM((SUBLANES, LANES), F32)],
        compiler_params=_params("arbitrary"), name="route",
    )(x, wr, br)


def _moe_plan(cum_incl, *, n_ck, nt, ni):
    ne, tf = N_EXPERTS, MOE_TILE
    i32 = jnp.int32
    ci = cum_incl[:, 0, :ne].astype(i32)
    ce = jnp.concatenate([jnp.zeros((1, ne), i32), ci[:-1]], axis=0)
    cnt = ci - ce
    gsz = ((ci[-1] + tf - 1) // tf) * tf
    gend = jnp.cumsum(gsz)
    goff = gend - gsz
    tiles = jnp.arange(nt, dtype=i32) * tf
    te = jnp.minimum(jnp.searchsorted(gend, tiles, side="right"), ne - 1).astype(i32)
    tv = (tiles < gend[-1]).astype(i32)

    a = (goff[None, :] + ce).T.reshape(-1)
    c = cnt.T.reshape(-1)
    t_lo = a // tf
    nrep = jnp.where(c > 0, (a + c - 1) // tf - t_lo + 1, 0)
    n_items = jnp.sum(nrep)
    start = jnp.cumsum(nrep) - nrep
    pair = jnp.repeat(jnp.arange(ne * n_ck, dtype=i32), nrep, total_repeat_length=ni)
    w = jnp.arange(ni, dtype=i32)
    valid = w < n_items
    it = t_lo[pair] + (w - start[pair])
    ick = pair % n_ck
    ie = pair // n_ck
    sub_lo = (jnp.maximum(a[pair], it * tf) - it * tf) // MOE_SUB
    sub_hi = (jnp.minimum(a[pair] + c[pair], (it + 1) * tf) - 1 - it * tf) // MOE_SUB
    sub_lo, sub_hi = (jnp.where(valid, v, 0).astype(i32) for v in (sub_lo, sub_hi))

    def finish(it, ick, ie, valid, by_tile):
        last = n_items - 1
        spare = gend[-1] // tf + (w - n_items)
        fill = by_tile & ~valid & (spare < nt)
        it = jnp.where(valid, it, jnp.minimum(spare, nt - 1) if by_tile else it[last])
        ick, ie = (jnp.where(valid, v, v[last]) for v in (ick, ie))
        g = it if by_tile else ick
        first = valid & (g != jnp.concatenate([jnp.full((1,), -1, i32), g[:-1]]))
        final = valid & ((g != jnp.concatenate([g[1:], jnp.full((1,), -1, i32)])) | (w == last))
        flags = valid * FLAG_VALID + (first | fill) * FLAG_FIRST + (final | fill) * FLAG_LAST
        return it.astype(i32), ick.astype(i32), ie.astype(i32), flags.astype(i32)

    by_tile = finish(it, ick, ie, valid, True)
    order = jnp.argsort(jnp.where(valid, ick * nt + it, jnp.iinfo(i32).max))
    by_chunk = finish(it[order], ick[order], ie[order], valid[order], False)
    return goff.astype(i32), te, tv, by_tile + (sub_lo, sub_hi), by_chunk


def _moe_gather_kernel(t_ref, ck_ref, e_ref, flag_ref, goff_ref, lo_ref, hi_ref, x_ref, rankt_ref, xs_ref, acc_ref):
    w = pl.program_id(0)
    flag = flag_ref[w]
    tf, tc = acc_ref.shape[0], x_ref.shape[0]

    @pl.when((flag & FLAG_FIRST) != 0)
    def _():
        acc_ref[...] = jnp.zeros_like(acc_ref)

    @pl.when((flag & FLAG_VALID) != 0)
    def _():
        e = e_ref[w]
        shift = (goff_ref[e] - t_ref[w] * tf).astype(F32)
        dest = rankt_ref[pl.ds(e, 1), :] + shift
        xb = x_ref[...].astype(BF16)

        def sub(sb, carry):
            r0 = pl.multiple_of(sb * MOE_SUB, MOE_SUB)
            row = (lax.broadcasted_iota(jnp.int32, (MOE_SUB, tc), 0) + r0).astype(F32)
            onehot = jnp.where(row == dest, 1.0, 0.0).astype(BF16)
            acc_ref[pl.ds(r0, MOE_SUB), :] += _mm(onehot, xb)
            return carry

        lax.fori_loop(lo_ref[w], hi_ref[w] + 1, sub, 0)

    @pl.when((flag & FLAG_LAST) != 0)
    def _():
        xs_ref[...] = acc_ref[...].astype(BF16)


def _moe_gather(plan, goff, x, rankt, *, nt):
    m, d = x.shape
    tf = tc = MOE_TILE
    t, ck, e, flags, sub_lo, sub_hi = plan
    grid_spec = pltpu.PrefetchScalarGridSpec(
        num_scalar_prefetch=7, grid=(t.shape[0],),
        in_specs=[pl.BlockSpec((tc, d), lambda w, t, ck, *_: (ck[w], 0)),
                  pl.BlockSpec((SUBLANES, tc), lambda w, t, ck, *_: (0, ck[w]))],
        out_specs=pl.BlockSpec((tf, d), lambda w, t, *_: (t[w], 0)),
        scratch_shapes=[pltpu.VMEM((tf, d), F32)],
    )
    return pl.pallas_call(
        _moe_gather_kernel, grid_spec=grid_spec, out_shape=jax.ShapeDtypeStruct((nt * tf, d), BF16),
        compiler_params=_params("arbitrary"), name="moe_gather",
    )(t, ck, e, flags, goff, sub_lo, sub_hi, x, rankt)


def _moe_ffn_kernel(te_ref, tv_ref, xs_ref, wg_ref, wu_ref, wd_ref, ys_ref, acc_ref):
    t = pl.program_id(0)
    f = pl.program_id(1)

    @pl.when(f == 0)
    def _():
        acc_ref[...] = jnp.zeros_like(acc_ref)

    @pl.when(tv_ref[t] != 0)
    def _():
        xs = xs_ref[...]
        h = _silu(_mm(xs, wg_ref[...])) * _mm(xs, wu_ref[...])
        acc_ref[...] += _mm(h, wd_ref[...])

    @pl.when(f == pl.num_programs(1) - 1)
    def _():
        ys_ref[...] = acc_ref[...].astype(BF16)


def _moe_ffn(te, tv, xs, wg, wu, wd):
    r, d = xs.shape
    dff = wg.shape[2]
    tf = MOE_TILE
    tff = _ffn_tile(dff)
    grid_spec = pltpu.PrefetchScalarGridSpec(
        num_scalar_prefetch=2, grid=(r // tf, dff // tff),
        in_specs=[pl.BlockSpec((tf, d), lambda t, f, te, tv: (t, 0)),
                  pl.BlockSpec((None, d, tff), lambda t, f, te, tv: (te[t], 0, f)),
                  pl.BlockSpec((None, d, tff), lambda t, f, te, tv: (te[t], 0, f)),
                  pl.BlockSpec((None, tff, d), lambda t, f, te, tv: (te[t], f, 0))],
        out_specs=pl.BlockSpec((tf, d), lambda t, f, te, tv: (t, 0)),
        scratch_shapes=[pltpu.VMEM((tf, d), F32)],
    )
    return pl.pallas_call(
        _moe_ffn_kernel, grid_spec=grid_spec, out_shape=jax.ShapeDtypeStruct((r, d), BF16),
        compiler_params=_params("arbitrary", "arbitrary"), name="moe_ffn",
    )(te, tv, xs, wg, wu, wd)


def _moe_combine_kernel(t_ref, ck_ref, e_ref, flag_ref, goff_ref, ys_ref, rank_ref, comb_ref, x_ref, g_ref, b_ref,
                        o_ref, acc_ref, *, alpha):
    w = pl.program_id(0)
    flag = flag_ref[w]
    tf, tc = ys_ref.shape[0], x_ref.shape[0]

    @pl.when((flag & FLAG_FIRST) != 0)
    def _():
        acc_ref[...] = jnp.zeros_like(acc_ref)

    @pl.when((flag & FLAG_VALID) != 0)
    def _():
        e = e_ref[w]
        shift = (goff_ref[e] - t_ref[w] * tf).astype(F32)
        mine = lax.broadcasted_iota(jnp.int32, (tc, LANES), 1) == e
        dest = jnp.sum(jnp.where(mine, rank_ref[...], 0.0), axis=1, keepdims=True) + shift
        weight = jnp.sum(jnp.where(mine, comb_ref[...], 0.0), axis=1, keepdims=True)
        col = lax.broadcasted_iota(jnp.int32, (tc, tf), 1).astype(F32)
        onehot = jnp.where(col == dest, 1.0, 0.0).astype(BF16)
        acc_ref[...] += weight * _mm(onehot, ys_ref[...])

    @pl.when((flag & FLAG_LAST) != 0)
    def _():
        o_ref[...] = _ln(alpha * x_ref[...] + acc_ref[...], g_ref[...], b_ref[...])


def _moe_combine(plan, goff, ys, rank, comb, x, g, b, *, alpha):
    m, d = x.shape
    tf = tc = MOE_TILE
    ni = plan[0].shape[0]
    chunk = lambda width: pl.BlockSpec((tc, width), lambda w, t, ck, e, fl, go: (ck[w], 0))
    full = lambda a: pl.BlockSpec(a.shape, lambda w, t, ck, e, fl, go: (0,) * a.ndim)
    grid_spec = pltpu.PrefetchScalarGridSpec(
        num_scalar_prefetch=5, grid=(ni,),
        in_specs=[pl.BlockSpec((tf, d), lambda w, t, ck, e, fl, go: (t[w], 0)), chunk(LANES), chunk(LANES), chunk(d),
                  full(g), full(b)],
        out_specs=chunk(d),
        scratch_shapes=[pltpu.VMEM((tc, d), F32)],
    )
    return pl.pallas_call(
        functools.partial(_moe_combine_kernel, alpha=alpha), grid_spec=grid_spec,
        out_shape=jax.ShapeDtypeStruct((m, d), F32),
        compiler_params=_params("arbitrary"), name="moe_combine",
    )(*plan, goff, ys, rank, comb, x, g, b)


def _moe_routed(x, x16, wr, br, wg, wu, wd, g, b, *, alpha):
    m = x.shape[0]
    tf = MOE_TILE
    n_ck = m // tf
    nt = -(-(2 * m + N_EXPERTS * (tf - 1)) // tf)
    ni = nt + N_EXPERTS * n_ck
    comb, rank, rankt, cum = _route(x, wr, br)
    goff, te, tv, by_tile, by_chunk = _moe_plan(cum, n_ck=n_ck, nt=nt, ni=ni)
    xs = _moe_gather(by_tile, goff, x16, rankt, nt=nt)
    ys = _moe_ffn(te, tv, xs, wg, wu, wd)
    return _moe_combine(by_chunk, goff, ys, rank, comb, x, g, b, alpha=alpha)


def _ple_kernel(x_ref, p_ref, wg_ref, wp_ref, g_ref, b_ref, o_ref, *, alpha):
    x = x_ref[...]
    gate = _sigmoid(_mm(x, wg_ref[...]))
    o_ref[...] = _ln(alpha * x + gate * _mm(p_ref[...], wp_ref[...]), g_ref[...], b_ref[...])


def _ple(x, p, wg, wp, g, b, *, alpha):
    m, d = x.shape
    tm = _row_tile(m, 512)
    full = lambda a: pl.BlockSpec(a.shape, lambda i: (0,) * a.ndim)
    return pl.pallas_call(
        functools.partial(_ple_kernel, alpha=alpha),
        grid=(m // tm,),
        in_specs=[pl.BlockSpec((tm, d), lambda i: (i, 0)), pl.BlockSpec((tm, p.shape[1]), lambda i: (i, 0)),
                  full(wg), full(wp), full(g), full(b)],
        out_specs=pl.BlockSpec((tm, d), lambda i: (i, 0)), out_shape=jax.ShapeDtypeStruct((m, d), F32),
        compiler_params=_params("arbitrary"), name="ple",
    )(x, p, wg, wp, g, b)


def kernel(x_prompt, x_sample, p_prompt, p_sample, cache_k, cache_v, state_conv_b, state_conv_d, page_table, w_in, w_gate, b_gate, a_ln_g, a_ln_b, a_w_s, a_b_s, b_conv_w, d_conv_w, d_conv_b, d_ln_g, d_ln_b, w_branch, w_out, ln_g, ln_b, ffn_w_gate, ffn_w_up, ffn_w_down, moe_w_router, moe_b_router, moe_w_gate, moe_w_up, moe_w_down, ple_w_gate, ple_w_proj):
    depth = w_in.shape[0]
    alpha = (2 * depth) ** 0.25
    nbatch, seq, d = x_prompt.shape
    ns, dec_seq, _ = x_sample.shape
    assert dec_seq == 1, "the sample group decodes one token per sequence"
    n_pool, page = cache_k.shape[1], cache_k.shape[2]
    n_pages = page_table.shape[1]
    assert MOBA_BLOCK % page == 0 and (n_pages * page) % MOBA_BLOCK == 0 and (n_pages * page) % CHUNK == 0
    ppb = MOBA_BLOCK // page
    nbp = n_pages // ppb
    assert nbp >= MOBA_TOPK
    pt_flat = page_table.reshape(-1).astype(jnp.int32)
    ck_all = jnp.transpose(cache_k, (0, 1, 3, 4, 2)).reshape(depth, n_pool, BW, page)
    cv_all = jnp.transpose(cache_v, (0, 1, 3, 4, 2)).reshape(depth, n_pool, BW, page)
    row2 = lambda a: a.reshape(1, -1)

    xp = x_prompt
    xs = x_sample.reshape(ns, d)
    outs = {k: [] for k in ("kp", "vp", "ks", "vs", "cbp", "cbs", "cdp", "cds", "cvs")}
    for i in range(depth):
        w_in_i = w_in[i].astype(BF16)
        w_gate_i = w_gate[i].astype(BF16)
        w_branch_i = w_branch[i].astype(BF16)
        w_out_i = w_out[i].astype(BF16)
        alng, alnb, dlng, dlnb, dcb = (row2(a[i]) for a in (a_ln_g, a_ln_b, d_ln_g, d_ln_b, d_conv_b))
        bg = row2(b_gate[i])

        abias = jnp.repeat(a_b_s[i].T, HEAD_DIM, axis=1)
        ya, yb, yd, q, k, v, kb, vt, km, tb, td = _inmix_prompt(
            xp, w_in_i, alng, alnb, a_w_s[i], abias, b_conv_w[i], d_conv_w[i], dcb, dlng, dlnb)
        yc = _attn_prompt(q, kb, vt, km.reshape(nbatch, seq // MOBA_BLOCK, BW))
        outs["kp"].append(k.reshape(nbatch, seq, HEADS, HEAD_DIM))
        outs["vp"].append(v.reshape(nbatch, seq, HEADS, HEAD_DIM))
        outs["cbp"].append(tb)
        outs["cdp"].append(td)
        m = nbatch * seq
        x1p, x1p16 = _merge(xp.reshape(m, d), ya.reshape(m, BW), yb.reshape(m, BW), yc.reshape(m, BW), yd.reshape(m, BW),
                     w_gate_i, bg, w_branch_i, w_out_i, row2(ln_g[i, 0]), row2(ln_b[i, 0]), alpha=alpha)

        w00 = row2(jnp.repeat(a_w_s[i][:, 0, 0], HEAD_DIM))
        b0 = row2(jnp.repeat(a_b_s[i][:, 0], HEAD_DIM))
        hb = jnp.transpose(state_conv_b[i], (1, 0, 2))
        hd = jnp.transpose(state_conv_d[i], (1, 0, 2))
        sya, syb, syd, sq, sk, sv, svn, scbn, scdn = _inmix_sample(
            xs, w_in[i], alng, alnb, w00, b0, b_conv_w[i], hb, d_conv_w[i], dcb, hd, dlng, dlnb)
        syc = _attn_sample(pt_flat, ck_all, cv_all, sq, sk, sv, layer=i, n_pages=n_pages, ppb=ppb)
        outs["ks"].append(sk.reshape(ns, 1, HEADS, HEAD_DIM))
        outs["vs"].append(sv.reshape(ns, 1, HEADS, HEAD_DIM))
        outs["cbs"].append(jnp.concatenate([state_conv_b[i][:, 1:], scbn[:, None]], axis=1))
        outs["cds"].append(jnp.concatenate([state_conv_d[i][:, 1:], scdn[:, None]], axis=1))
        outs["cvs"].append(svn.reshape(ns, 1, BW))
        x1s, _ = _merge(xs, sya, syb, syc.reshape(ns, BW), syd,
                     w_gate[i], bg, w_branch[i], w_out[i], row2(ln_g[i, 0]), row2(ln_b[i, 0]), alpha=alpha)

        g1, b1 = row2(ln_g[i, 1]), row2(ln_b[i, 1])
        j = i // 2
        if i % 2 == 0:
            wg, wu, wd = (w[j].astype(BF16) for w in (ffn_w_gate, ffn_w_up, ffn_w_down))
            x2p = _ffn(x1p, wg, wu, wd, g1, b1, alpha=alpha)
            x2s = _ffn(x1s, ffn_w_gate[j], ffn_w_up[j], ffn_w_down[j], g1, b1, alpha=alpha)
        else:
            wg, wu, wd = (w[j].astype(BF16) for w in (moe_w_gate, moe_w_up, moe_w_down))
            ne = moe_w_router.shape[2]
            wr = jnp.pad(moe_w_router[j], ((0, 0), (0, LANES - ne)))
            br = row2(jnp.pad(moe_b_router[j], (0, LANES - ne), constant_values=NEG))
            x2p = _moe_routed(x1p, x1p16, wr, br, wg, wu, wd, g1, b1, alpha=alpha)
            x2s = _moe(x1s, _router(x1s, wr, br), wg, wu, wd, g1, b1, alpha=alpha)

        wpg = ple_w_gate[i].astype(BF16)
        wpp = ple_w_proj[i].astype(BF16)
        g2, b2 = row2(ln_g[i, 2]), row2(ln_b[i, 2])
        xp = _ple(x2p, p_prompt[i].reshape(m, -1), wpg, wpp, g2, b2, alpha=alpha).reshape(nbatch, seq, d)
        xs = _ple(x2s, p_sample[i].reshape(ns, -1), ple_w_gate[i], ple_w_proj[i], g2, b2, alpha=alpha)

    st = lambda key: jnp.stack(outs[key])
    return (xp, xs.reshape(ns, 1, d), st("kp"), st("vp"), st("ks"), st("vs"), st("cbp"), st("cbs"), st("cdp"),
            st("cds"), st("cvs"))
```

```python
import functools
import math

import jax
import jax.numpy as jnp
from jax import lax
from jax.experimental import pallas as pl
from jax.experimental.pallas import tpu as pltpu

F32 = jnp.float32
BF16 = jnp.bfloat16
HIGHEST = lax.Precision.HIGHEST

BW = 256
N_BRANCH = 4
CHUNK = 128
HEADS = 4
HEAD_DIM = BW // HEADS
HEAD_SHIFT = 6
MOBA_BLOCK = 256
MOBA_TOPK = 3
B_CONV = 3
D_CONV = 31
N_EXPERTS = 8
LN_EPS = 1e-5
NEG = -1e30
LOG2E = math.log2(math.e)
LANES = 128
SUBLANES = 8
V7X_VMEM_LIMIT = 56 * 1024 * 1024
CONV_ROWS = 64
B_HALO = 8
D_HALO = 32


def _params(*sem):
    return pltpu.CompilerParams(dimension_semantics=sem, vmem_limit_bytes=V7X_VMEM_LIMIT)


def _ln(x, g, b):
    mu = jnp.mean(x, axis=-1, keepdims=True)
    xc = x - mu
    var = jnp.mean(xc * xc, axis=-1, keepdims=True)
    return xc * lax.rsqrt(var + LN_EPS) * g + b


def _gelu(x):
    c = math.sqrt(2.0 / math.pi)
    return x * (0.5 * (1.0 + jnp.tanh(c * (x + 0.044715 * (x * x * x)))))


def _sigmoid(x):
    return 1.0 / (1.0 + jnp.exp(-x))


def _silu(x):
    return x * _sigmoid(x)


def _mm(a, b):
    if b.dtype == F32:
        return jnp.dot(a.astype(F32), b, precision=HIGHEST, preferred_element_type=F32)
    return jnp.dot(a.astype(BF16), b, preferred_element_type=F32)


def _mm_nt(a, b, **kw):
    return lax.dot_general(a, b, (((1,), (1,)), ((), ())), preferred_element_type=F32, **kw)


def _lane_head(shape):
    return lax.shift_right_logical(lax.broadcasted_iota(jnp.int32, shape, len(shape) - 1), HEAD_SHIFT)


def _row_tile(m, want):
    tm = min(want, m)
    assert m % tm == 0, (m, tm)
    return tm


def _inmix_prompt_kernel(x_ref, w_ref, alng_ref, alnb_ref, aws_ref, abias_ref, bcw_ref, dcw_ref, dcb_ref,
                         dlng_ref, dlnb_ref,
                         ya_ref, yb_ref, yd_ref, q_ref, k_ref, v_ref, kb_ref, vt_ref, km_ref, tb_ref, td_ref,
                         cb_buf, cd_buf, *, tm, scale):
    t = pl.program_id(1)
    last = pl.num_programs(1) - 1

    @pl.when(t == 0)
    def _():
        cb_buf[0:B_HALO, :] = jnp.zeros((B_HALO, BW), F32)
        cd_buf[0:D_HALO, :] = jnp.zeros((D_HALO, BW), F32)
        cd_buf[D_HALO + tm:D_HALO + tm + SUBLANES, :] = jnp.zeros((SUBLANES, BW), F32)

    xb = x_ref[...].astype(BF16)

    def proj(j):
        return _mm(xb, w_ref[:, j * BW:(j + 1) * BW])

    gu = _gelu(proj(0))
    vnb = _ln(_gelu(proj(1)), alng_ref[...], alnb_ref[...]).astype(BF16)
    row = lax.broadcasted_iota(jnp.int32, (CHUNK, CHUNK), 0)
    col = lax.broadcasted_iota(jnp.int32, (CHUNK, CHUNK), 1)
    wts = [jnp.where(row >= col, aws_ref[h], 0.0).astype(BF16) for h in range(HEADS)]
    lane_head = _lane_head((CHUNK, BW))
    for c in range(tm // CHUNK):
        sl = slice(c * CHUNK, (c + 1) * CHUNK)
        vc = vnb[sl, :]
        s = abias_ref[...]
        for h in range(HEADS):
            s = s + _mm(wts[h], jnp.where(lane_head == h, vc, jnp.zeros_like(vc)))
        ya_ref[sl, :] = (gu[sl, :] * s).astype(BF16)

    bb = proj(2)
    cb_buf[B_HALO:B_HALO + tm, :] = proj(3) * proj(4)
    off_b = B_HALO - (B_CONV - 1)
    for r0 in range(0, tm, CONV_ROWS):
        acc = jnp.zeros((CONV_ROWS, BW), F32)
        for kk in range(B_CONV):
            acc = acc + bcw_ref[kk:kk + 1, :] * cb_buf[r0 + kk + off_b:r0 + kk + off_b + CONV_ROWS, :]
        yb_ref[r0:r0 + CONV_ROWS, :] = (bb[r0:r0 + CONV_ROWS, :] * acc).astype(BF16)

    @pl.when(t == last)
    def _():
        tb_ref[...] = cb_buf[tm + off_b:tm + B_HALO, :]

    cb_buf[0:B_HALO, :] = cb_buf[tm:tm + B_HALO, :]

    q_ref[...] = proj(5) * scale
    k = proj(6)
    v = proj(7)
    k_ref[...] = k
    v_ref[...] = v
    for i in range(tm // MOBA_BLOCK):
        sl = slice(i * MOBA_BLOCK, (i + 1) * MOBA_BLOCK)
        kblk = k[sl, :]
        kb_ref[i] = kblk.astype(BF16)
        vt_ref[i] = v[sl, :].T.astype(BF16)
        km_ref[i] = jnp.sum(kblk, axis=0, keepdims=True) * (1.0 / MOBA_BLOCK)

    cd_buf[D_HALO:D_HALO + tm, :] = proj(8) * _sigmoid(proj(9))
    off_d = D_HALO - (D_CONV - 1)
    span = CONV_ROWS + D_HALO + SUBLANES
    for r0 in range(0, tm, CONV_ROWS):
        acc = jnp.zeros((CONV_ROWS, BW), F32) + dcb_ref[...]
        block = cd_buf[r0:r0 + span, :]
        for r in range(SUBLANES):
            win = block if r == 0 else pltpu.roll(block, span - r, 0)
            for off in range(r, D_HALO + 1, SUBLANES):
                kk = off - off_d
                if 0 <= kk < D_CONV:
                    acc = acc + dcw_ref[kk:kk + 1, :] * win[off - r:off - r + CONV_ROWS, :]
        yd_ref[r0:r0 + CONV_ROWS, :] = _silu(_ln(acc, dlng_ref[...], dlnb_ref[...])).astype(BF16)

    @pl.when(t == last)
    def _():
        td_ref[...] = cd_buf[tm + off_d:tm + D_HALO, :]

    cd_buf[0:D_HALO, :] = cd_buf[tm:tm + D_HALO, :]


def _inmix_prompt(x, w_in, alng, alnb, aws, abias, bcw, dcw, dcb, dlng, dlnb):
    b, t, d = x.shape
    tm = _row_tile(t, 512)
    assert tm % MOBA_BLOCK == 0 and tm % CONV_ROWS == 0
    nbt = tm // MOBA_BLOCK
    nb = t // MOBA_BLOCK
    full = lambda shape: pl.BlockSpec(shape, lambda i, j: (0,) * len(shape))
    rows = lambda: pl.BlockSpec((None, tm, BW), lambda i, j: (i, j, 0))
    blocks = lambda: pl.BlockSpec((None, nbt, MOBA_BLOCK, MOBA_BLOCK), lambda i, j: (i, j, 0, 0))
    out_shape = (
        jax.ShapeDtypeStruct((b, t, BW), BF16),
        jax.ShapeDtypeStruct((b, t, BW), BF16),
        jax.ShapeDtypeStruct((b, t, BW), BF16),
        jax.ShapeDtypeStruct((b, t, BW), F32),
        jax.ShapeDtypeStruct((b, t, BW), F32),
        jax.ShapeDtypeStruct((b, t, BW), F32),
        jax.ShapeDtypeStruct((b, nb, MOBA_BLOCK, BW), BF16),
        jax.ShapeDtypeStruct((b, nb, BW, MOBA_BLOCK), BF16),
        jax.ShapeDtypeStruct((b, nb, 1, BW), F32),
        jax.ShapeDtypeStruct((b, B_CONV - 1, BW), F32),
        jax.ShapeDtypeStruct((b, D_CONV - 1, BW), F32),
    )
    out_specs = (
        rows(), rows(), rows(), rows(), rows(), rows(), blocks(), blocks(),
        pl.BlockSpec((None, nbt, 1, BW), lambda i, j: (i, j, 0, 0)),
        pl.BlockSpec((None, B_CONV - 1, BW), lambda i, j: (i, 0, 0)),
        pl.BlockSpec((None, D_CONV - 1, BW), lambda i, j: (i, 0, 0)),
    )
    in_specs = [
        pl.BlockSpec((None, tm, d), lambda i, j: (i, j, 0)),
        full(w_in.shape), full(alng.shape), full(alnb.shape), full(aws.shape), full(abias.shape),
        full(bcw.shape), full(dcw.shape), full(dcb.shape), full(dlng.shape), full(dlnb.shape),
    ]
    return pl.pallas_call(
        functools.partial(_inmix_prompt_kernel, tm=tm, scale=LOG2E / math.sqrt(HEAD_DIM)),
        grid=(b, t // tm), in_specs=in_specs, out_specs=out_specs, out_shape=out_shape,
        scratch_shapes=[pltpu.VMEM((B_HALO + tm, BW), F32), pltpu.VMEM((D_HALO + tm + SUBLANES, BW), F32)],
        compiler_params=_params("arbitrary", "arbitrary"), name="inmix_prompt",
    )(x, w_in, alng, alnb, aws, abias, bcw, dcw, dcb, dlng, dlnb)


def _inmix_sample_kernel(x_ref, w_ref, alng_ref, alnb_ref, w00_ref, b0_ref, bcw_ref, hb_ref, dcw_ref, dcb_ref,
                         hd_ref, dlng_ref, dlnb_ref,
                         ya_ref, yb_ref, yd_ref, q_ref, k_ref, v_ref, vn_ref, cbn_ref, cdn_ref, *, scale):
    x = x_ref[...]

    def proj(j):
        return _mm(x, w_ref[:, j * BW:(j + 1) * BW])

    vn = _ln(_gelu(proj(1)), alng_ref[...], alnb_ref[...])
    vn_ref[...] = vn
    ya_ref[...] = _gelu(proj(0)) * (vn * w00_ref[...] + b0_ref[...])

    cbn = proj(3) * proj(4)
    cbn_ref[...] = cbn
    conv = bcw_ref[B_CONV - 1:B_CONV, :] * cbn
    for kk in range(B_CONV - 1):
        conv = conv + bcw_ref[kk:kk + 1, :] * hb_ref[kk]
    yb_ref[...] = proj(2) * conv

    q_ref[...] = proj(5) * scale
    k_ref[...] = proj(6)
    v_ref[...] = proj(7)

    cdn = proj(8) * _sigmoid(proj(9))
    cdn_ref[...] = cdn
    conv = dcb_ref[...] + dcw_ref[D_CONV - 1:D_CONV, :] * cdn
    for kk in range(D_CONV - 1):
        conv = conv + dcw_ref[kk:kk + 1, :] * hd_ref[kk]
    yd_ref[...] = _silu(_ln(conv, dlng_ref[...], dlnb_ref[...]))


def _inmix_sample(x, w_in, alng, alnb, w00, b0, bcw, hb, dcw, dcb, hd, dlng, dlnb):
    s = x.shape[0]
    args = (x, w_in, alng, alnb, w00, b0, bcw, hb, dcw, dcb, hd, dlng, dlnb)
    full = lambda shape: pl.BlockSpec(shape, lambda i: (0,) * len(shape))
    row = lambda dt: jax.ShapeDtypeStruct((s, BW), dt)
    out_shape = tuple(row(F32) for _ in range(9))
    return pl.pallas_call(
        functools.partial(_inmix_sample_kernel, scale=1.0 / math.sqrt(HEAD_DIM)),
        grid=(1,), in_specs=[full(a.shape) for a in args], out_specs=tuple(full((s, BW)) for _ in out_shape),
        out_shape=out_shape, compiler_params=_params("arbitrary"), name="inmix_sample",
    )(*args)


def _attn_prompt_kernel(q_ref, kb_ref, vt_ref, km_ref, yc_ref, bias_scr, qh_scr, m_scr, l_scr, acc_scr, s_a, s_b,
                        *, nb):
    c = pl.program_id(1)
    qb = MOBA_BLOCK
    q = q_ref[...]
    q16 = q.astype(BF16)
    km = km_ref[...]
    lane_head_q = _lane_head((qb, BW))
    lane_head_km = _lane_head((nb, BW))
    blk = lax.broadcasted_iota(jnp.int32, (nb, qb), 0)
    key_pos = lax.broadcasted_iota(jnp.int32, (qb, qb), 0)
    q_pos = lax.broadcasted_iota(jnp.int32, (qb, qb), 1)
    valid = blk < c

    km_heads = jnp.concatenate([jnp.where(lane_head_km == h, km, 0.0) for h in range(HEADS)], axis=0)
    gate_all = _mm_nt(km_heads, q, precision=HIGHEST)

    for h in range(HEADS):
        g = jnp.where(valid, gate_all[h * nb:(h + 1) * nb, :], -jnp.inf)
        sel = jnp.zeros((nb, qb), F32)
        for _ in range(min(MOBA_TOPK, nb)):
            m = jnp.max(g, axis=0, keepdims=True)
            idx = jnp.min(jnp.where(g == m, blk, nb), axis=0, keepdims=True)
            hit = blk == idx
            sel = jnp.where(hit, 1.0, sel)
            g = jnp.where(hit, -jnp.inf, g)
        bias_scr[h] = jnp.where(jnp.logical_and(sel > 0.5, valid), 0.0, NEG)

        qh = jnp.where(lane_head_q == h, q16, jnp.zeros_like(q16))
        qh_scr[h * qb:(h + 1) * qb, :] = qh
        hs = slice(h * HEAD_DIM, (h + 1) * HEAD_DIM)
        s = jnp.where(key_pos <= q_pos, _mm_nt(kb_ref[c], qh), NEG)
        m = jnp.max(s, axis=0, keepdims=True)
        p = jnp.exp2(s - m)
        m_scr[h] = m
        l_scr[h] = jnp.sum(p, axis=0, keepdims=True)
        acc_scr[hs, :] = _mm(vt_ref[c, hs, :], p.astype(BF16))

    s_a[...] = _mm_nt(kb_ref[0], qh_scr[...])

    def step(j, src, dst):
        dst[...] = _mm_nt(kb_ref[jnp.minimum(j + 1, c - 1)], qh_scr[...])
        for h in range(HEADS):
            hs = slice(h * HEAD_DIM, (h + 1) * HEAD_DIM)
            s = src[:, h * qb:(h + 1) * qb] + bias_scr[h, pl.ds(j, 1), :]
            m = m_scr[h]
            m_new = jnp.maximum(m, jnp.max(s, axis=0, keepdims=True))
            a = jnp.exp2(m - m_new)
            p = jnp.exp2(s - m_new)
            m_scr[h] = m_new
            l_scr[h] = a * l_scr[h] + jnp.sum(p, axis=0, keepdims=True)
            acc_scr[hs, :] = a * acc_scr[hs, :] + _mm(vt_ref[j, hs, :], p.astype(BF16))

    def pair(i, carry):
        step(2 * i, s_a, s_b)
        step(2 * i + 1, s_b, s_a)
        return carry

    lax.fori_loop(0, lax.shift_right_logical(c, 1), pair, 0)

    @pl.when(lax.rem(c, 2) == 1)
    def _():
        step(c - 1, s_a, s_b)

    for h in range(HEADS):
        hs = slice(h * HEAD_DIM, (h + 1) * HEAD_DIM)
        acc_scr[hs, :] = acc_scr[hs, :] / l_scr[h]
    yc_ref[...] = acc_scr[...].T.astype(BF16)


def _attn_prompt(q, kb, vt, km):
    b, t, _ = q.shape
    nb = t // MOBA_BLOCK
    return pl.pallas_call(
        functools.partial(_attn_prompt_kernel, nb=nb),
        grid=(b, nb),
        in_specs=[
            pl.BlockSpec((None, MOBA_BLOCK, BW), lambda i, j: (i, j, 0)),
            pl.BlockSpec((None, nb, MOBA_BLOCK, BW), lambda i, j: (i, 0, 0, 0)),
            pl.BlockSpec((None, nb, BW, MOBA_BLOCK), lambda i, j: (i, 0, 0, 0)),
            pl.BlockSpec((None, nb, BW), lambda i, j: (i, 0, 0)),
        ],
        out_specs=pl.BlockSpec((None, MOBA_BLOCK, BW), lambda i, j: (i, j, 0)),
        out_shape=jax.ShapeDtypeStruct((b, t, BW), BF16),
        scratch_shapes=[pltpu.VMEM((HEADS, nb, MOBA_BLOCK), F32), pltpu.VMEM((HEADS * MOBA_BLOCK, BW), BF16),
                        pltpu.VMEM((HEADS, 1, MOBA_BLOCK), F32), pltpu.VMEM((HEADS, 1, MOBA_BLOCK), F32),
                        pltpu.VMEM((BW, MOBA_BLOCK), F32), pltpu.VMEM((MOBA_BLOCK, HEADS * MOBA_BLOCK), F32),
                        pltpu.VMEM((MOBA_BLOCK, HEADS * MOBA_BLOCK), F32)],
        compiler_params=_params("arbitrary", "arbitrary"), name="attn_prompt",
    )(q, kb, vt, km)


def _attn_sample_kernel(pt_ref, ck_ref, cv_ref, q_ref, kn_ref, vn_ref, o_ref, kbuf, vbuf, bsum, ksem, vsem,
                        *, layer, n_pages, ppb):
    b = pl.program_id(0)
    ns = pl.num_programs(0)
    slot = lax.rem(b, 2)
    nbp = n_pages // ppb
    per_head = MOBA_TOPK * ppb

    def k_copy(sample, i, sl):
        return pltpu.make_async_copy(ck_ref.at[layer, pt_ref[sample * n_pages + i]], kbuf.at[sl, i], ksem.at[sl])

    @pl.when(b == 0)
    def _():
        for i in range(n_pages):
            k_copy(0, i, 0).start()

    for i in range(n_pages):
        k_copy(b, i, slot).wait()

    @pl.when(b + 1 < ns)
    def _():
        for i in range(n_pages):
            k_copy(b + 1, i, 1 - slot).start()

    q = q_ref[...]
    group = SUBLANES

    def block_sums(jg, carry):
        rows = [[] for _ in range(HEADS)]
        for jj in range(group):
            tot = None
            for u in range(ppb):
                pr = kbuf[slot, (jg * group + jj) * ppb + u] * q
                tot = pr if tot is None else tot + pr
            for h in range(HEADS):
                rows[h].append(jnp.sum(tot[h * HEAD_DIM:(h + 1) * HEAD_DIM, :], axis=0, keepdims=True))
        for h in range(HEADS):
            bsum[h, pl.ds(pl.multiple_of(jg * group, group), group), :] = jnp.concatenate(rows[h], axis=0)
        return carry

    lax.fori_loop(0, nbp // group, block_sums, 0)

    blk = lax.broadcasted_iota(jnp.int32, (nbp, 1), 0)
    sel = []
    for h in range(HEADS):
        g = jnp.sum(bsum[h], axis=1, keepdims=True) * (1.0 / MOBA_BLOCK)
        for _ in range(MOBA_TOPK):
            idx = jnp.min(jnp.where(g == jnp.max(g), blk, nbp))
            sel.append(idx)
            g = jnp.where(blk == idx, -jnp.inf, g)

    def v_copy(h, r, u):
        n = h * per_head + r * ppb + u
        page = pt_ref[b * n_pages + sel[h * MOBA_TOPK + r] * ppb + u]
        return pltpu.make_async_copy(cv_ref.at[layer, page, pl.ds(h * HEAD_DIM, HEAD_DIM)], vbuf.at[n], vsem.at[0])

    v_copies = [v_copy(h, r, u) for h in range(HEADS) for r in range(MOBA_TOPK) for u in range(ppb)]
    for cp in v_copies:
        cp.start()

    kn = kn_ref[...]
    vn = vn_ref[...]
    probs = []
    for h in range(HEADS):
        hs = slice(h * HEAD_DIM, (h + 1) * HEAD_DIM)
        qh = q[hs, :]
        s_new = jnp.sum(qh * kn[hs, :], axis=0, keepdims=True)
        ss = []
        for r in range(MOBA_TOPK):
            for u in range(ppb):
                kp = kbuf[slot, sel[h * MOBA_TOPK + r] * ppb + u, hs, :]
                ss.append(jnp.sum(kp * qh, axis=0, keepdims=True))
        m = s_new
        for s in ss:
            m = jnp.maximum(m, jnp.max(s, axis=1, keepdims=True))
        p_new = jnp.exp(s_new - m)
        ps = [jnp.exp(s - m) for s in ss]
        l = p_new
        for p in ps:
            l = l + jnp.sum(p, axis=1, keepdims=True)
        probs.append((p_new, ps, l))

    for cp in v_copies:
        cp.wait()

    for h in range(HEADS):
        hs = slice(h * HEAD_DIM, (h + 1) * HEAD_DIM)
        p_new, ps, l = probs[h]
        o = p_new * vn[hs, :]
        for n, p in enumerate(ps):
            o = o + jnp.sum(vbuf[h * per_head + n] * p, axis=1, keepdims=True)
        o_ref[hs, :] = o / l


def _attn_sample(pt_flat, cache_k, cache_v, q, kn, vn, *, layer, n_pages, ppb):
    s = q.shape[0]
    page = cache_k.shape[3]
    n_sel = HEADS * MOBA_TOPK * ppb
    nbp = n_pages // ppb
    assert nbp % SUBLANES == 0
    col = lambda: pl.BlockSpec((None, BW, 1), lambda i, pt: (i, 0, 0))
    grid_spec = pltpu.PrefetchScalarGridSpec(
        num_scalar_prefetch=1, grid=(s,),
        in_specs=[pl.BlockSpec(memory_space=pl.ANY), pl.BlockSpec(memory_space=pl.ANY), col(), col(), col()],
        out_specs=col(),
        scratch_shapes=[pltpu.VMEM((2, n_pages, BW, page), F32), pltpu.VMEM((n_sel, HEAD_DIM, page), F32),
                        pltpu.VMEM((HEADS, nbp, page), F32),
                        pltpu.SemaphoreType.DMA((2,)), pltpu.SemaphoreType.DMA((1,))],
    )
    return pl.pallas_call(
        functools.partial(_attn_sample_kernel, layer=layer, n_pages=n_pages, ppb=ppb),
        grid_spec=grid_spec, out_shape=jax.ShapeDtypeStruct((s, BW, 1), F32),
        compiler_params=_params("arbitrary"), name="attn_sample",
    )(pt_flat, cache_k, cache_v, q[:, :, None], kn[:, :, None], vn[:, :, None])


def _merge_kernel(x_ref, ya_ref, yb_ref, yc_ref, yd_ref, wg_ref, bg_ref, wbr_ref, wo_ref, g_ref, b_ref, o_ref,
                  o16_ref, *, alpha):
    x = x_ref[...]
    xw = x.astype(wg_ref.dtype)
    d = x.shape[1]
    acc = None
    for j, y_ref in enumerate((ya_ref, yb_ref, yc_ref, yd_ref)):
        gate = _sigmoid(_mm(xw, wg_ref[:, j * d:(j + 1) * d]) + bg_ref[:, j * d:(j + 1) * d])
        term = gate * _mm(y_ref[...], wbr_ref[j])
        acc = term if acc is None else acc + term
    out = _mm(acc, wo_ref[...])
    res = _ln(alpha * x + out, g_ref[...], b_ref[...])
    o_ref[...] = res
    o16_ref[...] = res.astype(BF16)


def _merge(x, ya, yb, yc, yd, wg, bg, wbr, wo, g, b, *, alpha):
    m, d = x.shape
    tm = _row_tile(m, 512)
    full = lambda a: pl.BlockSpec(a.shape, lambda i: (0,) * a.ndim, pipeline_mode=pl.Buffered(1))
    rows = lambda w: pl.BlockSpec((tm, w), lambda i: (i, 0))
    return pl.pallas_call(
        functools.partial(_merge_kernel, alpha=alpha),
        grid=(m // tm,),
        in_specs=[rows(d), rows(BW), rows(BW), rows(BW), rows(BW), full(wg), full(bg), full(wbr), full(wo),
                  full(g), full(b)],
        out_specs=(rows(d), rows(d)),
        out_shape=(jax.ShapeDtypeStruct((m, d), F32), jax.ShapeDtypeStruct((m, d), BF16)),
        compiler_params=_params("arbitrary"), name="merge",
    )(x, ya, yb, yc, yd, wg, bg, wbr, wo, g, b)


def _ffn_kernel(x_ref, wg_ref, wu_ref, wd_ref, g_ref, b_ref, o_ref, acc_ref, *, alpha):
    f = pl.program_id(1)

    @pl.when(f == 0)
    def _():
        acc_ref[...] = jnp.zeros_like(acc_ref)

    xw = x_ref[...].astype(wg_ref.dtype)
    h = _silu(_mm(xw, wg_ref[...])) * _mm(xw, wu_ref[...])
    acc_ref[...] += _mm(h, wd_ref[...])

    @pl.when(f == pl.num_programs(1) - 1)
    def _():
        o_ref[...] = _ln(alpha * x_ref[...] + acc_ref[...], g_ref[...], b_ref[...])


def _ffn_tile(dff):
    return next(c for c in (1792, 1408, 1024, 896, 768, 512, 384, 256, 128) if dff % c == 0)


def _ffn(x, wg, wu, wd, g, b, *, alpha):
    m, d = x.shape
    dff = wg.shape[1]
    tm = _row_tile(m, 512)
    tf = _ffn_tile(dff)
    full = lambda a: pl.BlockSpec(a.shape, lambda i, f: (0,) * a.ndim)
    return pl.pallas_call(
        functools.partial(_ffn_kernel, alpha=alpha),
        grid=(m // tm, dff // tf),
        in_specs=[pl.BlockSpec((tm, d), lambda i, f: (i, 0)), pl.BlockSpec((d, tf), lambda i, f: (0, f)),
                  pl.BlockSpec((d, tf), lambda i, f: (0, f)), pl.BlockSpec((tf, d), lambda i, f: (f, 0)),
                  full(g), full(b)],
        out_specs=pl.BlockSpec((tm, d), lambda i, f: (i, 0)), out_shape=jax.ShapeDtypeStruct((m, d), F32),
        scratch_shapes=[pltpu.VMEM((tm, d), F32)],
        compiler_params=_params("arbitrary", "arbitrary"), name="ffn",
    )(x, wg, wu, wd, g, b)


def _router_kernel(x_ref, wr_ref, br_ref, o_ref):
    lg = jnp.dot(x_ref[...], wr_ref[...], precision=HIGHEST, preferred_element_type=F32) + br_ref[...]
    o_ref[...] = _top2(lg)[0]


def _router(x, wr, br):
    m, d = x.shape
    tm = _row_tile(m, 512)
    full = lambda a: pl.BlockSpec(a.shape, lambda i: (0,) * a.ndim)
    return pl.pallas_call(
        _router_kernel, grid=(m // tm,),
        in_specs=[pl.BlockSpec((tm, d), lambda i: (i, 0)), full(wr), full(br)],
        out_specs=pl.BlockSpec((tm, LANES), lambda i: (i, 0)), out_shape=jax.ShapeDtypeStruct((m, LANES), F32),
        compiler_params=_params("arbitrary"), name="router",
    )(x, wr, br)


def _moe_kernel(x_ref, c_ref, wg_ref, wu_ref, wd_ref, g_ref, b_ref, o_ref, acc_ref, *, alpha):
    e = pl.program_id(1)
    f = pl.program_id(2)

    @pl.when(jnp.logical_and(e == 0, f == 0))
    def _():
        acc_ref[...] = jnp.zeros_like(acc_ref)

    comb = c_ref[...]
    lane = lax.broadcasted_iota(jnp.int32, comb.shape, 1)
    ce = jnp.sum(jnp.where(lane == e, comb, 0.0), axis=1, keepdims=True)
    xw = x_ref[...].astype(wg_ref.dtype)
    h = _silu(_mm(xw, wg_ref[...])) * _mm(xw, wu_ref[...])
    acc_ref[...] += ce * _mm(h, wd_ref[...])

    @pl.when(jnp.logical_and(e == pl.num_programs(1) - 1, f == pl.num_programs(2) - 1))
    def _():
        o_ref[...] = _ln(alpha * x_ref[...] + acc_ref[...], g_ref[...], b_ref[...])


def _moe(x, comb, wg, wu, wd, g, b, *, alpha):
    m, d = x.shape
    ne, _, dff = wg.shape
    tm = _row_tile(m, 512)
    tf = _ffn_tile(dff)
    full = lambda a: pl.BlockSpec(a.shape, lambda i, e, f: (0,) * a.ndim)
    return pl.pallas_call(
        functools.partial(_moe_kernel, alpha=alpha),
        grid=(m // tm, ne, dff // tf),
        in_specs=[pl.BlockSpec((tm, d), lambda i, e, f: (i, 0)), pl.BlockSpec((tm, LANES), lambda i, e, f: (i, 0)),
                  pl.BlockSpec((None, d, tf), lambda i, e, f: (e, 0, f)),
                  pl.BlockSpec((None, d, tf), lambda i, e, f: (e, 0, f)),
                  pl.BlockSpec((None, tf, d), lambda i, e, f: (e, f, 0)), full(g), full(b)],
        out_specs=pl.BlockSpec((tm, d), lambda i, e, f: (i, 0)), out_shape=jax.ShapeDtypeStruct((m, d), F32),
        scratch_shapes=[pltpu.VMEM((tm, d), F32)],
        compiler_params=_params("arbitrary", "arbitrary", "arbitrary"), name="moe",
    )(x, comb, wg, wu, wd, g, b)


MOE_TILE = 512
MOE_SUB = 128
FLAG_VALID, FLAG_FIRST, FLAG_LAST = 1, 2, 4


def _top2(lg):
    lane = lax.broadcasted_iota(jnp.int32, lg.shape, 1)
    m1 = jnp.max(lg, axis=1, keepdims=True)
    hit1 = lane == jnp.min(jnp.where(lg == m1, lane, LANES), axis=1, keepdims=True)
    lg2 = jnp.where(hit1, -jnp.inf, lg)
    m2 = jnp.max(lg2, axis=1, keepdims=True)
    hit2 = lane == jnp.min(jnp.where(lg2 == m2, lane, LANES), axis=1, keepdims=True)
    e = jnp.exp(m2 - m1)
    comb = jnp.where(hit1, 1.0 / (1.0 + e), 0.0) + jnp.where(hit2, e / (1.0 + e), 0.0)
    return comb, jnp.logical_or(hit1, hit2)


def _route_kernel(x_ref, wr_ref, br_ref, comb_ref, rank_ref, rankt_ref, cum_ref, carry):
    @pl.when(pl.program_id(0) == 0)
    def _():
        carry[...] = jnp.zeros_like(carry)

    lg = jnp.dot(x_ref[...], wr_ref[...], precision=HIGHEST, preferred_element_type=F32) + br_ref[...]
    comb, chosen = _top2(lg)
    comb_ref[...] = comb
    sel = jnp.where(chosen, 1.0, 0.0)
    tc = sel.shape[0]
    before = lax.broadcasted_iota(jnp.int32, (tc, tc), 0) > lax.broadcasted_iota(jnp.int32, (tc, tc), 1)
    prefix = _mm(jnp.where(before, 1.0, 0.0).astype(BF16), sel.astype(BF16))
    rank = jnp.where(chosen, prefix + carry[0:1, :], -1.0)
    rank_ref[...] = rank
    rankt_ref[...] = rank.T[0:SUBLANES, :]
    carry[...] = carry[...] + jnp.sum(sel, axis=0, keepdims=True)
    cum_ref[...] = carry[...]


def _route(x, wr, br):
    m, d = x.shape
    tc = MOE_TILE
    assert m % tc == 0 and N_EXPERTS <= SUBLANES
    n_ck = m // tc
    full = lambda a: pl.BlockSpec(a.shape, lambda i: (0,) * a.ndim)
    return pl.pallas_call(
        _route_kernel, grid=(n_ck,),
        in_specs=[pl.BlockSpec((tc, d), lambda i: (i, 0)), full(wr), full(br)],
        out_specs=(pl.BlockSpec((tc, LANES), lambda i: (i, 0)), pl.BlockSpec((tc, LANES), lambda i: (i, 0)),
                   pl.BlockSpec((SUBLANES, tc), lambda i: (0, i)),
                   pl.BlockSpec((None, SUBLANES, LANES), lambda i: (i, 0, 0))),
        out_shape=(jax.ShapeDtypeStruct((m, LANES), F32), jax.ShapeDtypeStruct((m, LANES), F32),
                   jax.ShapeDtypeStruct((SUBLANES, m), F32), jax.ShapeDtypeStruct((n_ck, SUBLANES, LANES), F32)),
        scratch_shapes=[pltpu.VMEM((SUBLANES, LANES), F32)],
        compiler_params=_params("arbitrary"), name="route",
    )(x, wr, br)


def _moe_plan(cum_incl, *, n_ck, nt, ni):
    ne, tf = N_EXPERTS, MOE_TILE
    i32 = jnp.int32
    ci = cum_incl[:, 0, :ne].astype(i32)
    ce = jnp.concatenate([jnp.zeros((1, ne), i32), ci[:-1]], axis=0)
    cnt = ci - ce
    gsz = ((ci[-1] + tf - 1) // tf) * tf
    gend = jnp.cumsum(gsz)
    goff = gend - gsz
    tiles = jnp.arange(nt, dtype=i32) * tf
    te = jnp.minimum(jnp.searchsorted(gend, tiles, side="right"), ne - 1).astype(i32)
    tv = (tiles < gend[-1]).astype(i32)

    a = (goff[None, :] + ce).T.reshape(-1)
    c = cnt.T.reshape(-1)
    t_lo = a // tf
    nrep = jnp.where(c > 0, (a + c - 1) // tf - t_lo + 1, 0)
    n_items = jnp.sum(nrep)
    start = jnp.cumsum(nrep) - nrep
    pair = jnp.repeat(jnp.arange(ne * n_ck, dtype=i32), nrep, total_repeat_length=ni)
    w = jnp.arange(ni, dtype=i32)
    valid = w < n_items
    it = t_lo[pair] + (w - start[pair])
    ick = pair % n_ck
    ie = pair // n_ck
    sub_lo = (jnp.maximum(a[pair], it * tf) - it * tf) // MOE_SUB
    sub_hi = (jnp.minimum(a[pair] + c[pair], (it + 1) * tf) - 1 - it * tf) // MOE_SUB
    sub_lo, sub_hi = (jnp.where(valid, v, 0).astype(i32) for v in (sub_lo, sub_hi))

    def finish(it, ick, ie, valid, by_tile):
        last = n_items - 1
        spare = gend[-1] // tf + (w - n_items)
        fill = by_tile & ~valid & (spare < nt)
        it = jnp.where(valid, it, jnp.minimum(spare, nt - 1) if by_tile else it[last])
        ick, ie = (jnp.where(valid, v, v[last]) for v in (ick, ie))
        g = it if by_tile else ick
        first = valid & (g != jnp.concatenate([jnp.full((1,), -1, i32), g[:-1]]))
        final = valid & ((g != jnp.concatenate([g[1:], jnp.full((1,), -1, i32)])) | (w == last))
        flags = valid * FLAG_VALID + (first | fill) * FLAG_FIRST + (final | fill) * FLAG_LAST
        return it.astype(i32), ick.astype(i32), ie.astype(i32), flags.astype(i32)

    by_tile = finish(it, ick, ie, valid, True)
    order = jnp.argsort(jnp.where(valid, ick * nt + it, jnp.iinfo(i32).max))
    by_chunk = finish(it[order], ick[order], ie[order], valid[order], False)
    return goff.astype(i32), te, tv, by_tile + (sub_lo, sub_hi), by_chunk


def _moe_gather_kernel(t_ref, ck_ref, e_ref, flag_ref, goff_ref, lo_ref, hi_ref, x_ref, rankt_ref, xs_ref, acc_ref):
    w = pl.program_id(0)
    flag = flag_ref[w]
    tf, tc = acc_ref.shape[0], x_ref.shape[0]

    @pl.when((flag & FLAG_FIRST) != 0)
    def _():
        acc_ref[...] = jnp.zeros_like(acc_ref)

    @pl.when((flag & FLAG_VALID) != 0)
    def _():
        e = e_ref[w]
        shift = (goff_ref[e] - t_ref[w] * tf).astype(F32)
        dest = rankt_ref[pl.ds(e, 1), :] + shift
        xb = x_ref[...].astype(BF16)

        def sub(sb, carry):
            r0 = pl.multiple_of(sb * MOE_SUB, MOE_SUB)
            row = (lax.broadcasted_iota(jnp.int32, (MOE_SUB, tc), 0) + r0).astype(F32)
            onehot = jnp.where(row == dest, 1.0, 0.0).astype(BF16)
            acc_ref[pl.ds(r0, MOE_SUB), :] += _mm(onehot, xb)
            return carry

        lax.fori_loop(lo_ref[w], hi_ref[w] + 1, sub, 0)

    @pl.when((flag & FLAG_LAST) != 0)
    def _():
        xs_ref[...] = acc_ref[...].astype(BF16)


def _moe_gather(plan, goff, x, rankt, *, nt):
    m, d = x.shape
    tf = tc = MOE_TILE
    t, ck, e, flags, sub_lo, sub_hi = plan
    grid_spec = pltpu.PrefetchScalarGridSpec(
        num_scalar_prefetch=7, grid=(t.shape[0],),
        in_specs=[pl.BlockSpec((tc, d), lambda w, t, ck, *_: (ck[w], 0)),
                  pl.BlockSpec((SUBLANES, tc), lambda w, t, ck, *_: (0, ck[w]))],
        out_specs=pl.BlockSpec((tf, d), lambda w, t, *_: (t[w], 0)),
        scratch_shapes=[pltpu.VMEM((tf, d), F32)],
    )
    return pl.pallas_call(
        _moe_gather_kernel, grid_spec=grid_spec, out_shape=jax.ShapeDtypeStruct((nt * tf, d), BF16),
        compiler_params=_params("arbitrary"), name="moe_gather",
    )(t, ck, e, flags, goff, sub_lo, sub_hi, x, rankt)


def _moe_ffn_kernel(te_ref, tv_ref, xs_ref, wg_ref, wu_ref, wd_ref, ys_ref, acc_ref):
    t = pl.program_id(0)
    f = pl.program_id(1)

    @pl.when(f == 0)
    def _():
        acc_ref[...] = jnp.zeros_like(acc_ref)

    @pl.when(tv_ref[t] != 0)
    def _():
        xs = xs_ref[...]
        h = _silu(_mm(xs, wg_ref[...])) * _mm(xs, wu_ref[...])
        acc_ref[...] += _mm(h, wd_ref[...])

    @pl.when(f == pl.num_programs(1) - 1)
    def _():
        ys_ref[...] = acc_ref[...].astype(BF16)


def _moe_ffn(te, tv, xs, wg, wu, wd):
    r, d = xs.shape
    dff = wg.shape[2]
    tf = MOE_TILE
    tff = _ffn_tile(dff)
    grid_spec = pltpu.PrefetchScalarGridSpec(
        num_scalar_prefetch=2, grid=(r // tf, dff // tff),
        in_specs=[pl.BlockSpec((tf, d), lambda t, f, te, tv: (t, 0)),
                  pl.BlockSpec((None, d, tff), lambda t, f, te, tv: (te[t], 0, f)),
                  pl.BlockSpec((None, d, tff), lambda t, f, te, tv: (te[t], 0, f)),
                  pl.BlockSpec((None, tff, d), lambda t, f, te, tv: (te[t], f, 0))],
        out_specs=pl.BlockSpec((tf, d), lambda t, f, te, tv: (t, 0)),
        scratch_shapes=[pltpu.VMEM((tf, d), F32)],
    )
    return pl.pallas_call(
        _moe_ffn_kernel, grid_spec=grid_spec, out_shape=jax.ShapeDtypeStruct((r, d), BF16),
        compiler_params=_params("arbitrary", "arbitrary"), name="moe_ffn",
    )(te, tv, xs, wg, wu, wd)


def _moe_combine_kernel(t_ref, ck_ref, e_ref, flag_ref, goff_ref, ys_ref, rank_ref, comb_ref, x_ref, g_ref, b_ref,
                        o_ref, acc_ref, *, alpha):
    w = pl.program_id(0)
    flag = flag_ref[w]
    tf, tc = ys_ref.shape[0], x_ref.shape[0]

    @pl.when((flag & FLAG_FIRST) != 0)
    def _():
        acc_ref[...] = jnp.zeros_like(acc_ref)

    @pl.when((flag & FLAG_VALID) != 0)
    def _():
        e = e_ref[w]
        shift = (goff_ref[e] - t_ref[w] * tf).astype(F32)
        mine = lax.broadcasted_iota(jnp.int32, (tc, LANES), 1) == e
        dest = jnp.sum(jnp.where(mine, rank_ref[...], 0.0), axis=1, keepdims=True) + shift
        weight = jnp.sum(jnp.where(mine, comb_ref[...], 0.0), axis=1, keepdims=True)
        col = lax.broadcasted_iota(jnp.int32, (tc, tf), 1).astype(F32)
        onehot = jnp.where(col == dest, 1.0, 0.0).astype(BF16)
        acc_ref[...] += weight * _mm(onehot, ys_ref[...])

    @pl.when((flag & FLAG_LAST) != 0)
    def _():
        o_ref[...] = _ln(alpha * x_ref[...] + acc_ref[...], g_ref[...], b_ref[...])


def _moe_combine(plan, goff, ys, rank, comb, x, g, b, *, alpha):
    m, d = x.shape
    tf = tc = MOE_TILE
    ni = plan[0].shape[0]
    chunk = lambda width: pl.BlockSpec((tc, width), lambda w, t, ck, e, fl, go: (ck[w], 0))
    full = lambda a: pl.BlockSpec(a.shape, lambda w, t, ck, e, fl, go: (0,) * a.ndim)
    grid_spec = pltpu.PrefetchScalarGridSpec(
        num_scalar_prefetch=5, grid=(ni,),
        in_specs=[pl.BlockSpec((tf, d), lambda w, t, ck, e, fl, go: (t[w], 0)), chunk(LANES), chunk(LANES), chunk(d),
                  full(g), full(b)],
        out_specs=chunk(d),
        scratch_shapes=[pltpu.VMEM((tc, d), F32)],
    )
    return pl.pallas_call(
        functools.partial(_moe_combine_kernel, alpha=alpha), grid_spec=grid_spec,
        out_shape=jax.ShapeDtypeStruct((m, d), F32),
        compiler_params=_params("arbitrary"), name="moe_combine",
    )(*plan, goff, ys, rank, comb, x, g, b)


def _moe_routed(x, x16, wr, br, wg, wu, wd, g, b, *, alpha):
    m = x.shape[0]
    tf = MOE_TILE
    n_ck = m // tf
    nt = -(-(2 * m + N_EXPERTS * (tf - 1)) // tf)
    ni = nt + N_EXPERTS * n_ck
    comb, rank, rankt, cum = _route(x, wr, br)
    goff, te, tv, by_tile, by_chunk = _moe_plan(cum, n_ck=n_ck, nt=nt, ni=ni)
    xs = _moe_gather(by_tile, goff, x16, rankt, nt=nt)
    ys = _moe_ffn(te, tv, xs, wg, wu, wd)
    return _moe_combine(by_chunk, goff, ys, rank, comb, x, g, b, alpha=alpha)


def _ple_kernel(x_ref, p_ref, wg_ref, wp_ref, g_ref, b_ref, o_ref, *, alpha):
    x = x_ref[...]
    gate = _sigmoid(_mm(x, wg_ref[...]))
    o_ref[...] = _ln(alpha * x + gate * _mm(p_ref[...], wp_ref[...]), g_ref[...], b_ref[...])


def _ple(x, p, wg, wp, g, b, *, alpha):
    m, d = x.shape
    tm = _row_tile(m, 512)
    full = lambda a: pl.BlockSpec(a.shape, lambda i: (0,) * a.ndim)
    return pl.pallas_call(
        functools.partial(_ple_kernel, alpha=alpha),
        grid=(m // tm,),
        in_specs=[pl.BlockSpec((tm, d), lambda i: (i, 0)), pl.BlockSpec((tm, p.shape[1]), lambda i: (i, 0)),
                  full(wg), full(wp), full(g), full(b)],
        out_specs=pl.BlockSpec((tm, d), lambda i: (i, 0)), out_shape=jax.ShapeDtypeStruct((m, d), F32),
        compiler_params=_params("arbitrary"), name="ple",
    )(x, p, wg, wp, g, b)


def kernel(x_prompt, x_sample, p_prompt, p_sample, cache_k, cache_v, state_conv_b, state_conv_d, page_table, w_in, w_gate, b_gate, a_ln_g, a_ln_b, a_w_s, a_b_s, b_conv_w, d_conv_w, d_conv_b, d_ln_g, d_ln_b, w_branch, w_out, ln_g, ln_b, ffn_w_gate, ffn_w_up, ffn_w_down, moe_w_router, moe_b_router, moe_w_gate, moe_w_up, moe_w_down, ple_w_gate, ple_w_proj):
    depth = w_in.shape[0]
    alpha = (2 * depth) ** 0.25
    nbatch, seq, d = x_prompt.shape
    ns, dec_seq, _ = x_sample.shape
    assert dec_seq == 1, "the sample group decodes one token per sequence"
    n_pool, page = cache_k.shape[1], cache_k.shape[2]
    n_pages = page_table.shape[1]
    assert MOBA_BLOCK % page == 0 and (n_pages * page) % MOBA_BLOCK == 0 and (n_pages * page) % CHUNK == 0
    ppb = MOBA_BLOCK // page
    nbp = n_pages // ppb
    assert nbp >= MOBA_TOPK
    pt_flat = page_table.reshape(-1).astype(jnp.int32)
    ck_all = jnp.transpose(cache_k, (0, 1, 3, 4, 2)).reshape(depth, n_pool, BW, page)
    cv_all = jnp.transpose(cache_v, (0, 1, 3, 4, 2)).reshape(depth, n_pool, BW, page)
    row2 = lambda a: a.reshape(1, -1)

    xp = x_prompt
    xs = x_sample.reshape(ns, d)
    outs = {k: [] for k in ("kp", "vp", "ks", "vs", "cbp", "cbs", "cdp", "cds", "cvs")}
    for i in range(depth):
        w_in_i = w_in[i].astype(BF16)
        w_gate_i = w_gate[i].astype(BF16)
        w_branch_i = w_branch[i].astype(BF16)
        w_out_i = w_out[i].astype(BF16)
        alng, alnb, dlng, dlnb, dcb = (row2(a[i]) for a in (a_ln_g, a_ln_b, d_ln_g, d_ln_b, d_conv_b))
        bg = row2(b_gate[i])

        abias = jnp.repeat(a_b_s[i].T, HEAD_DIM, axis=1)
        ya, yb, yd, q, k, v, kb, vt, km, tb, td = _inmix_prompt(
            xp, w_in_i, alng, alnb, a_w_s[i], abias, b_conv_w[i], d_conv_w[i], dcb, dlng, dlnb)
        yc = _attn_prompt(q, kb, vt, km.reshape(nbatch, seq // MOBA_BLOCK, BW))
        outs["kp"].append(k.reshape(nbatch, seq, HEADS, HEAD_DIM))
        outs["vp"].append(v.reshape(nbatch, seq, HEADS, HEAD_DIM))
        outs["cbp"].append(tb)
        outs["cdp"].append(td)
        m = nbatch * seq
        x1p, x1p16 = _merge(xp.reshape(m, d), ya.reshape(m, BW), yb.reshape(m, BW), yc.reshape(m, BW), yd.reshape(m, BW),
                     w_gate_i, bg, w_branch_i, w_out_i, row2(ln_g[i, 0]), row2(ln_b[i, 0]), alpha=alpha)

        w00 = row2(jnp.repeat(a_w_s[i][:, 0, 0], HEAD_DIM))
        b0 = row2(jnp.repeat(a_b_s[i][:, 0], HEAD_DIM))
        hb = jnp.transpose(state_conv_b[i], (1, 0, 2))
        hd = jnp.transpose(state_conv_d[i], (1, 0, 2))
        sya, syb, syd, sq, sk, sv, svn, scbn, scdn = _inmix_sample(
            xs, w_in[i], alng, alnb, w00, b0, b_conv_w[i], hb, d_conv_w[i], dcb, hd, dlng, dlnb)
        syc = _attn_sample(pt_flat, ck_all, cv_all, sq, sk, sv, layer=i, n_pages=n_pages, ppb=ppb)
        outs["ks"].append(sk.reshape(ns, 1, HEADS, HEAD_DIM))
        outs["vs"].append(sv.reshape(ns, 1, HEADS, HEAD_DIM))
        outs["cbs"].append(jnp.concatenate([state_conv_b[i][:, 1:], scbn[:, None]], axis=1))
        outs["cds"].append(jnp.concatenate([state_conv_d[i][:, 1:], scdn[:, None]], axis=1))
        outs["cvs"].append(svn.reshape(ns, 1, BW))
        x1s, _ = _merge(xs, sya, syb, syc.reshape(ns, BW), syd,
                     w_gate[i], bg, w_branch[i], w_out[i], row2(ln_g[i, 0]), row2(ln_b[i, 0]), alpha=alpha)

        g1, b1 = row2(ln_g[i, 1]), row2(ln_b[i, 1])
        j = i // 2
        if i % 2 == 0:
            wg, wu, wd = (w[j].astype(BF16) for w in (ffn_w_gate, ffn_w_up, ffn_w_down))
            x2p = _ffn(x1p, wg, wu, wd, g1, b1, alpha=alpha)
            x2s = _ffn(x1s, ffn_w_gate[j], ffn_w_up[j], ffn_w_down[j], g1, b1, alpha=alpha)
        else:
            wg, wu, wd = (w[j].astype(BF16) for w in (moe_w_gate, moe_w_up, moe_w_down))
            ne = moe_w_router.shape[2]
            wr = jnp.pad(moe_w_router[j], ((0, 0), (0, LANES - ne)))
            br = row2(jnp.pad(moe_b_router[j], (0, LANES - ne), constant_values=NEG))
            x2p = _moe_routed(x1p, x1p16, wr, br, wg, wu, wd, g1, b1, alpha=alpha)
            x2s = _moe(x1s, _router(x1s, wr, br), wg, wu, wd, g1, b1, alpha=alpha)

        wpg = ple_w_gate[i].astype(BF16)
        wpp = ple_w_proj[i].astype(BF16)
        g2, b2 = row2(ln_g[i, 2]), row2(ln_b[i, 2])
        xp = _ple(x2p, p_prompt[i].reshape(m, -1), wpg, wpp, g2, b2, alpha=alpha).reshape(nbatch, seq, d)
        xs = _ple(x2s, p_sample[i].reshape(ns, -1), ple_w_gate[i], ple_w_proj[i], g2, b2, alpha=alpha)

    st = lambda key: jnp.stack(outs[key])
    return (xp, xs.reshape(ns, 1, d), st("kp"), st("vp"), st("ks"), st("vs"), st("cbp"), st("cbs"), st("cdp"),
            st("cds"), st("cvs"))
```

```python
import functools
import math

import jax
import jax.numpy as jnp
from jax import lax
from jax.experimental import pallas as pl
from jax.experimental.pallas import tpu as pltpu

F32 = jnp.float32
BF16 = jnp.bfloat16
HIGHEST = lax.Precision.HIGHEST

BW = 256
N_BRANCH = 4
CHUNK = 128
HEADS = 4
HEAD_DIM = BW // HEADS
HEAD_SHIFT = 6
MOBA_BLOCK = 256
MOBA_TOPK = 3
B_CONV = 3
D_CONV = 31
N_EXPERTS = 8
LN_EPS = 1e-5
NEG = -1e30
LOG2E = math.log2(math.e)
LANES = 128
SUBLANES = 8
V7X_VMEM_LIMIT = 56 * 1024 * 1024
CONV_ROWS = 64
B_HALO = 8
D_HALO = 32


def _params(*sem):
    return pltpu.CompilerParams(dimension_semantics=sem, vmem_limit_bytes=V7X_VMEM_LIMIT)


def _ln(x, g, b):
    mu = jnp.mean(x, axis=-1, keepdims=True)
    xc = x - mu
    var = jnp.mean(xc * xc, axis=-1, keepdims=True)
    return xc * lax.rsqrt(var + LN_EPS) * g + b


def _gelu(x):
    c = math.sqrt(2.0 / math.pi)
    return x * (0.5 * (1.0 + jnp.tanh(c * (x + 0.044715 * (x * x * x)))))


def _sigmoid(x):
    return 1.0 / (1.0 + jnp.exp(-x))


def _silu(x):
    return x * _sigmoid(x)


def _mm(a, b):
    if b.dtype == F32:
        return jnp.dot(a.astype(F32), b, precision=HIGHEST, preferred_element_type=F32)
    return jnp.dot(a.astype(BF16), b, preferred_element_type=F32)


def _mm_nt(a, b, **kw):
    return lax.dot_general(a, b, (((1,), (1,)), ((), ())), preferred_element_type=F32, **kw)


def _lane_head(shape):
    return lax.shift_right_logical(lax.broadcasted_iota(jnp.int32, shape, len(shape) - 1), HEAD_SHIFT)


def _row_tile(m, want):
    tm = min(want, m)
    assert m % tm == 0, (m, tm)
    return tm


def _inmix_prompt_kernel(x_ref, w_ref, alng_ref, alnb_ref, aws_ref, abias_ref, bcw_ref, dcw_ref, dcb_ref,
                         dlng_ref, dlnb_ref,
                         ya_ref, yb_ref, yd_ref, q_ref, k_ref, v_ref, kb_ref, vt_ref, km_ref, tb_ref, td_ref,
                         cb_buf, cd_buf, *, tm, scale):
    t = pl.program_id(1)
    last = pl.num_programs(1) - 1

    @pl.when(t == 0)
    def _():
        cb_buf[0:B_HALO, :] = jnp.zeros((B_HALO, BW), F32)
        cd_buf[0:D_HALO, :] = jnp.zeros((D_HALO, BW), F32)
        cd_buf[D_HALO + tm:D_HALO + tm + SUBLANES, :] = jnp.zeros((SUBLANES, BW), F32)

    xb = x_ref[...].astype(BF16)

    def proj(j):
        return _mm(xb, w_ref[:, j * BW:(j + 1) * BW])

    gu = _gelu(proj(0))
    vnb = _ln(_gelu(proj(1)), alng_ref[...], alnb_ref[...]).astype(BF16)
    row = lax.broadcasted_iota(jnp.int32, (CHUNK, CHUNK), 0)
    col = lax.broadcasted_iota(jnp.int32, (CHUNK, CHUNK), 1)
    wts = [jnp.where(row >= col, aws_ref[h], 0.0).astype(BF16) for h in range(HEADS)]
    lane_head = _lane_head((CHUNK, BW))
    for c in range(tm // CHUNK):
        sl = slice(c * CHUNK, (c + 1) * CHUNK)
        vc = vnb[sl, :]
        s = abias_ref[...]
        for h in range(HEADS):
            s = s + _mm(wts[h], jnp.where(lane_head == h, vc, jnp.zeros_like(vc)))
        ya_ref[sl, :] = (gu[sl, :] * s).astype(BF16)

    bb = proj(2)
    cb_buf[B_HALO:B_HALO + tm, :] = proj(3) * proj(4)
    off_b = B_HALO - (B_CONV - 1)
    for r0 in range(0, tm, CONV_ROWS):
        acc = jnp.zeros((CONV_ROWS, BW), F32)
        for kk in range(B_CONV):
            acc = acc + bcw_ref[kk:kk + 1, :] * cb_buf[r0 + kk + off_b:r0 + kk + off_b + CONV_ROWS, :]
        yb_ref[r0:r0 + CONV_ROWS, :] = (bb[r0:r0 + CONV_ROWS, :] * acc).astype(BF16)

    @pl.when(t == last)
    def _():
        tb_ref[...] = cb_buf[tm + off_b:tm + B_HALO, :]

    cb_buf[0:B_HALO, :] = cb_buf[tm:tm + B_HALO, :]

    q_ref[...] = proj(5) * scale
    k = proj(6)
    v = proj(7)
    k_ref[...] = k
    v_ref[...] = v
    for i in range(tm // MOBA_BLOCK):
        sl = slice(i * MOBA_BLOCK, (i + 1) * MOBA_BLOCK)
        kblk = k[sl, :]
        kb_ref[i] = kblk.astype(BF16)
        vt_ref[i] = v[sl, :].T.astype(BF16)
        km_ref[i] = jnp.sum(kblk, axis=0, keepdims=True) * (1.0 / MOBA_BLOCK)

    cd_buf[D_HALO:D_HALO + tm, :] = proj(8) * _sigmoid(proj(9))
    off_d = D_HALO - (D_CONV - 1)
    span = CONV_ROWS + D_HALO + SUBLANES
    for r0 in range(0, tm, CONV_ROWS):
        acc = jnp.zeros((CONV_ROWS, BW), F32) + dcb_ref[...]
        block = cd_buf[r0:r0 + span, :]
        for r in range(SUBLANES):
            win = block if r == 0 else pltpu.roll(block, span - r, 0)
            for off in range(r, D_HALO + 1, SUBLANES):
                kk = off - off_d
                if 0 <= kk < D_CONV:
                    acc = acc + dcw_ref[kk:kk + 1, :] * win[off - r:off - r + CONV_ROWS, :]
        yd_ref[r0:r0 + CONV_ROWS, :] = _silu(_ln(acc, dlng_ref[...], dlnb_ref[...])).astype(BF16)

    @pl.when(t == last)
    def _():
        td_ref[...] = cd_buf[tm + off_d:tm + D_HALO, :]

    cd_buf[0:D_HALO, :] = cd_buf[tm:tm + D_HALO, :]


def _inmix_prompt(x, w_in, alng, alnb, aws, abias, bcw, dcw, dcb, dlng, dlnb):
    b, t, d = x.shape
    tm = _row_tile(t, 512)
    assert tm % MOBA_BLOCK == 0 and tm % CONV_ROWS == 0
    nbt = tm // MOBA_BLOCK
    nb = t // MOBA_BLOCK
    full = lambda shape: pl.BlockSpec(shape, lambda i, j: (0,) * len(shape))
    rows = lambda: pl.BlockSpec((None, tm, BW), lambda i, j: (i, j, 0))
    blocks = lambda: pl.BlockSpec((None, nbt, MOBA_BLOCK, MOBA_BLOCK), lambda i, j: (i, j, 0, 0))
    out_shape = (
        jax.ShapeDtypeStruct((b, t, BW), BF16),
        jax.ShapeDtypeStruct((b, t, BW), BF16),
        jax.ShapeDtypeStruct((b, t, BW), BF16),
        jax.ShapeDtypeStruct((b, t, BW), F32),
        jax.ShapeDtypeStruct((b, t, BW), F32),
        jax.ShapeDtypeStruct((b, t, BW), F32),
        jax.ShapeDtypeStruct((b, nb, MOBA_BLOCK, BW), BF16),
        jax.ShapeDtypeStruct((b, nb, BW, MOBA_BLOCK), BF16),
        jax.ShapeDtypeStruct((b, nb, 1, BW), F32),
        jax.ShapeDtypeStruct((b, B_CONV - 1, BW), F32),
        jax.ShapeDtypeStruct((b, D_CONV - 1, BW), F32),
    )
    out_specs = (
        rows(), rows(), rows(), rows(), rows(), rows(), blocks(), blocks(),
        pl.BlockSpec((None, nbt, 1, BW), lambda i, j: (i, j, 0, 0)),
        pl.BlockSpec((None, B_CONV - 1, BW), lambda i, j: (i, 0, 0)),
        pl.BlockSpec((None, D_CONV - 1, BW), lambda i, j: (i, 0, 0)),
    )
    in_specs = [
        pl.BlockSpec((None, tm, d), lambda i, j: (i, j, 0)),
        full(w_in.shape), full(alng.shape), full(alnb.shape), full(aws.shape), full(abias.shape),
        full(bcw.shape), full(dcw.shape), full(dcb.shape), full(dlng.shape), full(dlnb.shape),
    ]
    return pl.pallas_call(
        functools.partial(_inmix_prompt_kernel, tm=tm, scale=LOG2E / math.sqrt(HEAD_DIM)),
        grid=(b, t // tm), in_specs=in_specs, out_specs=out_specs, out_shape=out_shape,
        scratch_shapes=[pltpu.VMEM((B_HALO + tm, BW), F32), pltpu.VMEM((D_HALO + tm + SUBLANES, BW), F32)],
        compiler_params=_params("arbitrary", "arbitrary"), name="inmix_prompt",
    )(x, w_in, alng, alnb, aws, abias, bcw, dcw, dcb, dlng, dlnb)


def _inmix_sample_kernel(x_ref, w_ref, alng_ref, alnb_ref, w00_ref, b0_ref, bcw_ref, hb_ref, dcw_ref, dcb_ref,
                         hd_ref, dlng_ref, dlnb_ref,
                         ya_ref, yb_ref, yd_ref, q_ref, k_ref, v_ref, vn_ref, cbn_ref, cdn_ref, *, scale):
    x = x_ref[...]

    def proj(j):
        return _mm(x, w_ref[:, j * BW:(j + 1) * BW])

    vn = _ln(_gelu(proj(1)), alng_ref[...], alnb_ref[...])
    vn_ref[...] = vn
    ya_ref[...] = _gelu(proj(0)) * (vn * w00_ref[...] + b0_ref[...])

    cbn = proj(3) * proj(4)
    cbn_ref[...] = cbn
    conv = bcw_ref[B_CONV - 1:B_CONV, :] * cbn
    for kk in range(B_CONV - 1):
        conv = conv + bcw_ref[kk:kk + 1, :] * hb_ref[kk]
    yb_ref[...] = proj(2) * conv

    q_ref[...] = proj(5) * scale
    k_ref[...] = proj(6)
    v_ref[...] = proj(7)

    cdn = proj(8) * _sigmoid(proj(9))
    cdn_ref[...] = cdn
    conv = dcb_ref[...] + dcw_ref[D_CONV - 1:D_CONV, :] * cdn
    for kk in range(D_CONV - 1):
        conv = conv + dcw_ref[kk:kk + 1, :] * hd_ref[kk]
    yd_ref[...] = _silu(_ln(conv, dlng_ref[...], dlnb_ref[...]))


def _inmix_sample(x, w_in, alng, alnb, w00, b0, bcw, hb, dcw, dcb, hd, dlng, dlnb):
    s = x.shape[0]
    args = (x, w_in, alng, alnb, w00, b0, bcw, hb, dcw, dcb, hd, dlng, dlnb)
    full = lambda shape: pl.BlockSpec(shape, lambda i: (0,) * len(shape))
    row = lambda dt: jax.ShapeDtypeStruct((s, BW), dt)
    out_shape = tuple(row(F32) for _ in range(9))
    return pl.pallas_call(
        functools.partial(_inmix_sample_kernel, scale=1.0 / math.sqrt(HEAD_DIM)),
        grid=(1,), in_specs=[full(a.shape) for a in args], out_specs=tuple(full((s, BW)) for _ in out_shape),
        out_shape=out_shape, compiler_params=_params("arbitrary"), name="inmix_sample",
    )(*args)


def _attn_prompt_kernel(q_ref, kb_ref, vt_ref, km_ref, yc_ref, bias_scr, qh_scr, m_scr, l_scr, acc_scr, s_a, s_b,
                        *, nb):
    c = pl.program_id(1)
    qb = MOBA_BLOCK
    q = q_ref[...]
    q16 = q.astype(BF16)
    km = km_ref[...]
    lane_head_q = _lane_head((qb, BW))
    lane_head_km = _lane_head((nb, BW))
    blk = lax.broadcasted_iota(jnp.int32, (nb, qb), 0)
    key_pos = lax.broadcasted_iota(jnp.int32, (qb, qb), 0)
    q_pos = lax.broadcasted_iota(jnp.int32, (qb, qb), 1)
    valid = blk < c

    km_heads = jnp.concatenate([jnp.where(lane_head_km == h, km, 0.0) for h in range(HEADS)], axis=0)
    gate_all = _mm_nt(km_heads, q, precision=HIGHEST)

    for h in range(HEADS):
        g = jnp.where(valid, gate_all[h * nb:(h + 1) * nb, :], -jnp.inf)
        sel = jnp.zeros((nb, qb), F32)
        for _ in range(min(MOBA_TOPK, nb)):
            m = jnp.max(g, axis=0, keepdims=True)
            idx = jnp.min(jnp.where(g == m, blk, nb), axis=0, keepdims=True)
            hit = blk == idx
            sel = jnp.where(hit, 1.0, sel)
            g = jnp.where(hit, -jnp.inf, g)
        bias_scr[h] = jnp.where(jnp.logical_and(sel > 0.5, valid), 0.0, NEG)

        qh = jnp.where(lane_head_q == h, q16, jnp.zeros_like(q16))
        qh_scr[h * qb:(h + 1) * qb, :] = qh
        hs = slice(h * HEAD_DIM, (h + 1) * HEAD_DIM)
        s = jnp.where(key_pos <= q_pos, _mm_nt(kb_ref[c], qh), NEG)
        m = jnp.max(s, axis=0, keepdims=True)
        p = jnp.exp2(s - m)
        m_scr[h] = m
        l_scr[h] = jnp.sum(p, axis=0, keepdims=True)
        acc_scr[hs, :] = _mm(vt_ref[c, hs, :], p.astype(BF16))

    s_a[...] = _mm_nt(kb_ref[0], qh_scr[...])

    def step(j, src, dst):
        dst[...] = _mm_nt(kb_ref[jnp.minimum(j + 1, c - 1)], qh_scr[...])
        for h in range(HEADS):
            hs = slice(h * HEAD_DIM, (h + 1) * HEAD_DIM)
            s = src[:, h * qb:(h + 1) * qb] + bias_scr[h, pl.ds(j, 1), :]
            m = m_scr[h]
            m_new = jnp.maximum(m, jnp.max(s, axis=0, keepdims=True))
            a = jnp.exp2(m - m_new)
            p = jnp.exp2(s - m_new)
            m_scr[h] = m_new
            l_scr[h] = a * l_scr[h] + jnp.sum(p, axis=0, keepdims=True)
            acc_scr[hs, :] = a * acc_scr[hs, :] + _mm(vt_ref[j, hs, :], p.astype(BF16))

    def pair(i, carry):
        step(2 * i, s_a, s_b)
        step(2 * i + 1, s_b, s_a)
        return carry

    lax.fori_loop(0, lax.shift_right_logical(c, 1), pair, 0)

    @pl.when(lax.rem(c, 2) == 1)
    def _():
        step(c - 1, s_a, s_b)

    for h in range(HEADS):
        hs = slice(h * HEAD_DIM, (h + 1) * HEAD_DIM)
        acc_scr[hs, :] = acc_scr[hs, :] / l_scr[h]
    yc_ref[...] = acc_scr[...].T.astype(BF16)


def _attn_prompt(q, kb, vt, km):
    b, t, _ = q.shape
    nb = t // MOBA_BLOCK
    return pl.pallas_call(
        functools.partial(_attn_prompt_kernel, nb=nb),
        grid=(b, nb),
        in_specs=[
            pl.BlockSpec((None, MOBA_BLOCK, BW), lambda i, j: (i, j, 0)),
            pl.BlockSpec((None, nb, MOBA_BLOCK, BW), lambda i, j: (i, 0, 0, 0)),
            pl.BlockSpec((None, nb, BW, MOBA_BLOCK), lambda i, j: (i, 0, 0, 0)),
            pl.BlockSpec((None, nb, BW), lambda i, j: (i, 0, 0)),
        ],
        out_specs=pl.BlockSpec((None, MOBA_BLOCK, BW), lambda i, j: (i, j, 0)),
        out_shape=jax.ShapeDtypeStruct((b, t, BW), BF16),
        scratch_shapes=[pltpu.VMEM((HEADS, nb, MOBA_BLOCK), F32), pltpu.VMEM((HEADS * MOBA_BLOCK, BW), BF16),
                        pltpu.VMEM((HEADS, 1, MOBA_BLOCK), F32), pltpu.VMEM((HEADS, 1, MOBA_BLOCK), F32),
                        pltpu.VMEM((BW, MOBA_BLOCK), F32), pltpu.VMEM((MOBA_BLOCK, HEADS * MOBA_BLOCK), F32),
                        pltpu.VMEM((MOBA_BLOCK, HEADS * MOBA_BLOCK), F32)],
        compiler_params=_params("arbitrary", "arbitrary"), name="attn_prompt",
    )(q, kb, vt, km)


def _attn_sample_kernel(pt_ref, ck_ref, cv_ref, q_ref, kn_ref, vn_ref, o_ref, kbuf, vbuf, bsum, ksem, vsem,
                        *, layer, n_pages, ppb):
    b = pl.program_id(0)
    ns = pl.num_programs(0)
    slot = lax.rem(b, 2)
    nbp = n_pages // ppb
    per_head = MOBA_TOPK * ppb

    def k_copy(sample, i, sl):
        return pltpu.make_async_copy(ck_ref.at[layer, pt_ref[sample * n_pages + i]], kbuf.at[sl, i], ksem.at[sl])

    @pl.when(b == 0)
    def _():
        for i in range(n_pages):
            k_copy(0, i, 0).start()

    for i in range(n_pages):
        k_copy(b, i, slot).wait()

    @pl.when(b + 1 < ns)
    def _():
        for i in range(n_pages):
            k_copy(b + 1, i, 1 - slot).start()

    q = q_ref[...]
    group = SUBLANES

    def block_sums(jg, carry):
        rows = [[] for _ in range(HEADS)]
        for jj in range(group):
            tot = None
            for u in range(ppb):
                pr = kbuf[slot, (jg * group + jj) * ppb + u] * q
                tot = pr if tot is None else tot + pr
            for h in range(HEADS):
                rows[h].append(jnp.sum(tot[h * HEAD_DIM:(h + 1) * HEAD_DIM, :], axis=0, keepdims=True))
        for h in range(HEADS):
            bsum[h, pl.ds(pl.multiple_of(jg * group, group), group), :] = jnp.concatenate(rows[h], axis=0)
        return carry

    lax.fori_loop(0, nbp // group, block_sums, 0)

    blk = lax.broadcasted_iota(jnp.int32, (nbp, 1), 0)
    sel = []
    for h in range(HEADS):
        g = jnp.sum(bsum[h], axis=1, keepdims=True) * (1.0 / MOBA_BLOCK)
        for _ in range(MOBA_TOPK):
            idx = jnp.min(jnp.where(g == jnp.max(g), blk, nbp))
            sel.append(idx)
            g = jnp.where(blk == idx, -jnp.inf, g)

    def v_copy(h, r, u):
        n = h * per_head + r * ppb + u
        page = pt_ref[b * n_pages + sel[h * MOBA_TOPK + r] * ppb + u]
        return pltpu.make_async_copy(cv_ref.at[layer, page, pl.ds(h * HEAD_DIM, HEAD_DIM)], vbuf.at[n], vsem.at[0])

    v_copies = [v_copy(h, r, u) for h in range(HEADS) for r in range(MOBA_TOPK) for u in range(ppb)]
    for cp in v_copies:
        cp.start()

    kn = kn_ref[...]
    vn = vn_ref[...]
    probs = []
    for h in range(HEADS):
        hs = slice(h * HEAD_DIM, (h + 1) * HEAD_DIM)
        qh = q[hs, :]
        s_new = jnp.sum(qh * kn[hs, :], axis=0, keepdims=True)
        ss = []
        for r in range(MOBA_TOPK):
            for u in range(ppb):
                kp = kbuf[slot, sel[h * MOBA_TOPK + r] * ppb + u, hs, :]
                ss.append(jnp.sum(kp * qh, axis=0, keepdims=True))
        m = s_new
        for s in ss:
            m = jnp.maximum(m, jnp.max(s, axis=1, keepdims=True))
        p_new = jnp.exp(s_new - m)
        ps = [jnp.exp(s - m) for s in ss]
        l = p_new
        for p in ps:
            l = l + jnp.sum(p, axis=1, keepdims=True)
        probs.append((p_new, ps, l))

    for cp in v_copies:
        cp.wait()

    for h in range(HEADS):
        hs = slice(h * HEAD_DIM, (h + 1) * HEAD_DIM)
        p_new, ps, l = probs[h]
        o = p_new * vn[hs, :]
        for n, p in enumerate(ps):
            o = o + jnp.sum(vbuf[h * per_head + n] * p, axis=1, keepdims=True)
        o_ref[hs, :] = o / l


def _attn_sample(pt_flat, cache_k, cache_v, q, kn, vn, *, layer, n_pages, ppb):
    s = q.shape[0]
    page = cache_k.shape[3]
    n_sel = HEADS * MOBA_TOPK * ppb
    nbp = n_pages // ppb
    assert nbp % SUBLANES == 0
    col = lambda: pl.BlockSpec((None, BW, 1), lambda i, pt: (i, 0, 0))
    grid_spec = pltpu.PrefetchScalarGridSpec(
        num_scalar_prefetch=1, grid=(s,),
        in_specs=[pl.BlockSpec(memory_space=pl.ANY), pl.BlockSpec(memory_space=pl.ANY), col(), col(), col()],
        out_specs=col(),
        scratch_shapes=[pltpu.VMEM((2, n_pages, BW, page), F32), pltpu.VMEM((n_sel, HEAD_DIM, page), F32),
                        pltpu.VMEM((HEADS, nbp, page), F32),
                        pltpu.SemaphoreType.DMA((2,)), pltpu.SemaphoreType.DMA((1,))],
    )
    return pl.pallas_call(
        functools.partial(_attn_sample_kernel, layer=layer, n_pages=n_pages, ppb=ppb),
        grid_spec=grid_spec, out_shape=jax.ShapeDtypeStruct((s, BW, 1), F32),
        compiler_params=_params("arbitrary"), name="attn_sample",
    )(pt_flat, cache_k, cache_v, q[:, :, None], kn[:, :, None], vn[:, :, None])


def _merge_kernel(x_ref, ya_ref, yb_ref, yc_ref, yd_ref, wg_ref, bg_ref, wbr_ref, wo_ref, g_ref, b_ref, o_ref,
                  o16_ref, *, alpha):
    x = x_ref[...]
    xw = x.astype(wg_ref.dtype)
    d = x.shape[1]
    acc = None
    for j, y_ref in enumerate((ya_ref, yb_ref, yc_ref, yd_ref)):
        gate = _sigmoid(_mm(xw, wg_ref[:, j * d:(j + 1) * d]) + bg_ref[:, j * d:(j + 1) * d])
        term = gate * _mm(y_ref[...], wbr_ref[j])
        acc = term if acc is None else acc + term
    out = _mm(acc, wo_ref[...])
    res = _ln(alpha * x + out, g_ref[...], b_ref[...])
    o_ref[...] = res
    o16_ref[...] = res.astype(BF16)


def _merge(x, ya, yb, yc, yd, wg, bg, wbr, wo, g, b, *, alpha):
    m, d = x.shape
    tm = _row_tile(m, 512)
    full = lambda a: pl.BlockSpec(a.shape, lambda i: (0,) * a.ndim, pipeline_mode=pl.Buffered(1))
    rows = lambda w: pl.BlockSpec((tm, w), lambda i: (i, 0))
    return pl.pallas_call(
        functools.partial(_merge_kernel, alpha=alpha),
        grid=(m // tm,),
        in_specs=[rows(d), rows(BW), rows(BW), rows(BW), rows(BW), full(wg), full(bg), full(wbr), full(wo),
                  full(g), full(b)],
        out_specs=(rows(d), rows(d)),
        out_shape=(jax.ShapeDtypeStruct((m, d), F32), jax.ShapeDtypeStruct((m, d), BF16)),
        compiler_params=_params("arbitrary"), name="merge",
    )(x, ya, yb, yc, yd, wg, bg, wbr, wo, g, b)


def _ffn_kernel(x_ref, wg_ref, wu_ref, wd_ref, g_ref, b_ref, o_ref, acc_ref, *, alpha):
    f = pl.program_id(1)

    @pl.when(f == 0)
    def _():
        acc_ref[...] = jnp.zeros_like(acc_ref)

    xw = x_ref[...].astype(wg_ref.dtype)
    h = _silu(_mm(xw, wg_ref[...])) * _mm(xw, wu_ref[...])
    acc_ref[...] += _mm(h, wd_ref[...])

    @pl.when(f == pl.num_programs(1) - 1)
    def _():
        o_ref[...] = _ln(alpha * x_ref[...] + acc_ref[...], g_ref[...], b_ref[...])


def _ffn_tile(dff):
    return next(c for c in (1792, 1408, 1024, 896, 768, 512, 384, 256, 128) if dff % c == 0)


def _ffn(x, wg, wu, wd, g, b, *, alpha):
    m, d = x.shape
    dff = wg.shape[1]
    tm = _row_tile(m, 512)
    tf = dff
    full = lambda a: pl.BlockSpec(a.shape, lambda i, f: (0,) * a.ndim)
    resident = pl.Buffered(1)
    return pl.pallas_call(
        functools.partial(_ffn_kernel, alpha=alpha),
        grid=(m // tm, dff // tf),
        in_specs=[pl.BlockSpec((tm, d), lambda i, f: (i, 0)),
                  pl.BlockSpec((d, tf), lambda i, f: (0, f), pipeline_mode=resident),
                  pl.BlockSpec((d, tf), lambda i, f: (0, f), pipeline_mode=resident),
                  pl.BlockSpec((tf, d), lambda i, f: (f, 0), pipeline_mode=resident),
                  full(g), full(b)],
        out_specs=pl.BlockSpec((tm, d), lambda i, f: (i, 0)), out_shape=jax.ShapeDtypeStruct((m, d), F32),
        scratch_shapes=[pltpu.VMEM((tm, d), F32)],
        compiler_params=_params("arbitrary", "arbitrary"), name="ffn",
    )(x, wg, wu, wd, g, b)


def _router_kernel(x_ref, wr_ref, br_ref, o_ref):
    lg = jnp.dot(x_ref[...], wr_ref[...], precision=HIGHEST, preferred_element_type=F32) + br_ref[...]
    o_ref[...] = _top2(lg)[0]


def _router(x, wr, br):
    m, d = x.shape
    tm = _row_tile(m, 512)
    full = lambda a: pl.BlockSpec(a.shape, lambda i: (0,) * a.ndim)
    return pl.pallas_call(
        _router_kernel, grid=(m // tm,),
        in_specs=[pl.BlockSpec((tm, d), lambda i: (i, 0)), full(wr), full(br)],
        out_specs=pl.BlockSpec((tm, LANES), lambda i: (i, 0)), out_shape=jax.ShapeDtypeStruct((m, LANES), F32),
        compiler_params=_params("arbitrary"), name="router",
    )(x, wr, br)


def _moe_kernel(x_ref, c_ref, wg_ref, wu_ref, wd_ref, g_ref, b_ref, o_ref, acc_ref, *, alpha):
    e = pl.program_id(1)
    f = pl.program_id(2)

    @pl.when(jnp.logical_and(e == 0, f == 0))
    def _():
        acc_ref[...] = jnp.zeros_like(acc_ref)

    comb = c_ref[...]
    lane = lax.broadcasted_iota(jnp.int32, comb.shape, 1)
    ce = jnp.sum(jnp.where(lane == e, comb, 0.0), axis=1, keepdims=True)
    xw = x_ref[...].astype(wg_ref.dtype)
    h = _silu(_mm(xw, wg_ref[...])) * _mm(xw, wu_ref[...])
    acc_ref[...] += ce * _mm(h, wd_ref[...])

    @pl.when(jnp.logical_and(e == pl.num_programs(1) - 1, f == pl.num_programs(2) - 1))
    def _():
        o_ref[...] = _ln(alpha * x_ref[...] + acc_ref[...], g_ref[...], b_ref[...])


def _moe(x, comb, wg, wu, wd, g, b, *, alpha):
    m, d = x.shape
    ne, _, dff = wg.shape
    tm = _row_tile(m, 512)
    tf = _ffn_tile(dff)
    full = lambda a: pl.BlockSpec(a.shape, lambda i, e, f: (0,) * a.ndim)
    return pl.pallas_call(
        functools.partial(_moe_kernel, alpha=alpha),
        grid=(m // tm, ne, dff // tf),
        in_specs=[pl.BlockSpec((tm, d), lambda i, e, f: (i, 0)), pl.BlockSpec((tm, LANES), lambda i, e, f: (i, 0)),
                  pl.BlockSpec((None, d, tf), lambda i, e, f: (e, 0, f)),
                  pl.BlockSpec((None, d, tf), lambda i, e, f: (e, 0, f)),
                  pl.BlockSpec((None, tf, d), lambda i, e, f: (e, f, 0)), full(g), full(b)],
        out_specs=pl.BlockSpec((tm, d), lambda i, e, f: (i, 0)), out_shape=jax.ShapeDtypeStruct((m, d), F32),
        scratch_shapes=[pltpu.VMEM((tm, d), F32)],
        compiler_params=_params("arbitrary", "arbitrary", "arbitrary"), name="moe",
    )(x, comb, wg, wu, wd, g, b)


MOE_TILE = 512
MOE_SUB = 128
FLAG_VALID, FLAG_FIRST, FLAG_LAST = 1, 2, 4


def _top2(lg):
    lane = lax.broadcasted_iota(jnp.int32, lg.shape, 1)
    m1 = jnp.max(lg, axis=1, keepdims=True)
    hit1 = lane == jnp.min(jnp.where(lg == m1, lane, LANES), axis=1, keepdims=True)
    lg2 = jnp.where(hit1, -jnp.inf, lg)
    m2 = jnp.max(lg2, axis=1, keepdims=True)
    hit2 = lane == jnp.min(jnp.where(lg2 == m2, lane, LANES), axis=1, keepdims=True)
    e = jnp.exp(m2 - m1)
    comb = jnp.where(hit1, 1.0 / (1.0 + e), 0.0) + jnp.where(hit2, e / (1.0 + e), 0.0)
    return comb, jnp.logical_or(hit1, hit2)


def _route_kernel(x_ref, wr_ref, br_ref, comb_ref, rank_ref, rankt_ref, cum_ref, carry):
    @pl.when(pl.program_id(0) == 0)
    def _():
        carry[...] = jnp.zeros_like(carry)

    lg = jnp.dot(x_ref[...], wr_ref[...], precision=HIGHEST, preferred_element_type=F32) + br_ref[...]
    comb, chosen = _top2(lg)
    comb_ref[...] = comb
    sel = jnp.where(chosen, 1.0, 0.0)
    tc = sel.shape[0]
    before = lax.broadcasted_iota(jnp.int32, (tc, tc), 0) > lax.broadcasted_iota(jnp.int32, (tc, tc), 1)
    prefix = _mm(jnp.where(before, 1.0, 0.0).astype(BF16), sel.astype(BF16))
    rank = jnp.where(chosen, prefix + carry[0:1, :], -1.0)
    rank_ref[...] = rank
    rankt_ref[...] = rank.T[0:SUBLANES, :]
    carry[...] = carry[...] + jnp.sum(sel, axis=0, keepdims=True)
    cum_ref[...] = carry[...]


def _route(x, wr, br):
    m, d = x.shape
    tc = MOE_TILE
    assert m % tc == 0 and N_EXPERTS <= SUBLANES
    n_ck = m // tc
    full = lambda a: pl.BlockSpec(a.shape, lambda i: (0,) * a.ndim)
    return pl.pallas_call(
        _route_kernel, grid=(n_ck,),
        in_specs=[pl.BlockSpec((tc, d), lambda i: (i, 0)), full(wr), full(br)],
        out_specs=(pl.BlockSpec((tc, LANES), lambda i: (i, 0)), pl.BlockSpec((tc, LANES), lambda i: (i, 0)),
                   pl.BlockSpec((SUBLANES, tc), lambda i: (0, i)),
                   pl.BlockSpec((None, SUBLANES, LANES), lambda i: (i, 0, 0))),
        out_shape=(jax.ShapeDtypeStruct((m, LANES), F32), jax.ShapeDtypeStruct((m, LANES), F32),
                   jax.ShapeDtypeStruct((SUBLANES, m), F32), jax.ShapeDtypeStruct((n_ck, SUBLANES, LANES), F32)),
        scratch_shapes=[pltpu.VMEM((SUBLANES, LANES), F32)],
        compiler_params=_params("arbitrary"), name="route",
    )(x, wr, br)


def _moe_plan(cum_incl, *, n_ck, nt, ni):
    ne, tf = N_EXPERTS, MOE_TILE
    i32 = jnp.int32
    ci = cum_incl[:, 0, :ne].astype(i32)
    ce = jnp.concatenate([jnp.zeros((1, ne), i32), ci[:-1]], axis=0)
    cnt = ci - ce
    gsz = ((ci[-1] + tf - 1) // tf) * tf
    gend = jnp.cumsum(gsz)
    goff = gend - gsz
    tiles = jnp.arange(nt, dtype=i32) * tf
    te = jnp.minimum(jnp.searchsorted(gend, tiles, side="right"), ne - 1).astype(i32)
    tv = (tiles < gend[-1]).astype(i32)

    a = (goff[None, :] + ce).T.reshape(-1)
    c = cnt.T.reshape(-1)
    t_lo = a // tf
    nrep = jnp.where(c > 0, (a + c - 1) // tf - t_lo + 1, 0)
    n_items = jnp.sum(nrep)
    start = jnp.cumsum(nrep) - nrep
    pair = jnp.repeat(jnp.arange(ne * n_ck, dtype=i32), nrep, total_repeat_length=ni)
    w = jnp.arange(ni, dtype=i32)
    valid = w < n_items
    it = t_lo[pair] + (w - start[pair])
    ick = pair % n_ck
    ie = pair // n_ck
    sub_lo = (jnp.maximum(a[pair], it * tf) - it * tf) // MOE_SUB
    sub_hi = (jnp.minimum(a[pair] + c[pair], (it + 1) * tf) - 1 - it * tf) // MOE_SUB
    sub_lo, sub_hi = (jnp.where(valid, v, 0).astype(i32) for v in (sub_lo, sub_hi))

    def finish(it, ick, ie, valid, by_tile):
        last = n_items - 1
        spare = gend[-1] // tf + (w - n_items)
        fill = by_tile & ~valid & (spare < nt)
        it = jnp.where(valid, it, jnp.minimum(spare, nt - 1) if by_tile else it[last])
        ick, ie = (jnp.where(valid, v, v[last]) for v in (ick, ie))
        g = it if by_tile else ick
        first = valid & (g != jnp.concatenate([jnp.full((1,), -1, i32), g[:-1]]))
        final = valid & ((g != jnp.concatenate([g[1:], jnp.full((1,), -1, i32)])) | (w == last))
        flags = valid * FLAG_VALID + (first | fill) * FLAG_FIRST + (final | fill) * FLAG_LAST
        return it.astype(i32), ick.astype(i32), ie.astype(i32), flags.astype(i32)

    by_tile = finish(it, ick, ie, valid, True)
    order = jnp.argsort(jnp.where(valid, ick * nt + it, jnp.iinfo(i32).max))
    by_chunk = finish(it[order], ick[order], ie[order], valid[order], False)
    return goff.astype(i32), te, tv, by_tile + (sub_lo, sub_hi), by_chunk


def _moe_gather_kernel(t_ref, ck_ref, e_ref, flag_ref, goff_ref, lo_ref, hi_ref, x_ref, rankt_ref, xs_ref, acc_ref):
    w = pl.program_id(0)
    flag = flag_ref[w]
    tf, tc = acc_ref.shape[0], x_ref.shape[0]

    @pl.when((flag & FLAG_FIRST) != 0)
    def _():
        acc_ref[...] = jnp.zeros_like(acc_ref)

    @pl.when((flag & FLAG_VALID) != 0)
    def _():
        e = e_ref[w]
        shift = (goff_ref[e] - t_ref[w] * tf).astype(F32)
        dest = rankt_ref[pl.ds(e, 1), :] + shift
        xb = x_ref[...].astype(BF16)

        def sub(sb, carry):
            r0 = pl.multiple_of(sb * MOE_SUB, MOE_SUB)
            row = (lax.broadcasted_iota(jnp.int32, (MOE_SUB, tc), 0) + r0).astype(F32)
            onehot = jnp.where(row == dest, 1.0, 0.0).astype(BF16)
            acc_ref[pl.ds(r0, MOE_SUB), :] += _mm(onehot, xb)
            return carry

        lax.fori_loop(lo_ref[w], hi_ref[w] + 1, sub, 0)

    @pl.when((flag & FLAG_LAST) != 0)
    def _():
        xs_ref[...] = acc_ref[...].astype(BF16)


def _moe_gather(plan, goff, x, rankt, *, nt):
    m, d = x.shape
    tf = tc = MOE_TILE
    t, ck, e, flags, sub_lo, sub_hi = plan
    grid_spec = pltpu.PrefetchScalarGridSpec(
        num_scalar_prefetch=7, grid=(t.shape[0],),
        in_specs=[pl.BlockSpec((tc, d), lambda w, t, ck, *_: (ck[w], 0)),
                  pl.BlockSpec((SUBLANES, tc), lambda w, t, ck, *_: (0, ck[w]))],
        out_specs=pl.BlockSpec((tf, d), lambda w, t, *_: (t[w], 0)),
        scratch_shapes=[pltpu.VMEM((tf, d), F32)],
    )
    return pl.pallas_call(
        _moe_gather_kernel, grid_spec=grid_spec, out_shape=jax.ShapeDtypeStruct((nt * tf, d), BF16),
        compiler_params=_params("arbitrary"), name="moe_gather",
    )(t, ck, e, flags, goff, sub_lo, sub_hi, x, rankt)


def _moe_ffn_kernel(te_ref, tv_ref, xs_ref, wg_ref, wu_ref, wd_ref, ys_ref, acc_ref):
    t = pl.program_id(0)
    f = pl.program_id(1)

    @pl.when(f == 0)
    def _():
        acc_ref[...] = jnp.zeros_like(acc_ref)

    @pl.when(tv_ref[t] != 0)
    def _():
        xs = xs_ref[...]
        h = _silu(_mm(xs, wg_ref[...])) * _mm(xs, wu_ref[...])
        acc_ref[...] += _mm(h, wd_ref[...])

    @pl.when(f == pl.num_programs(1) - 1)
    def _():
        ys_ref[...] = acc_ref[...].astype(BF16)


def _moe_ffn(te, tv, xs, wg, wu, wd):
    r, d = xs.shape
    dff = wg.shape[2]
    tf = MOE_TILE
    tff = _ffn_tile(dff)
    grid_spec = pltpu.PrefetchScalarGridSpec(
        num_scalar_prefetch=2, grid=(r // tf, dff // tff),
        in_specs=[pl.BlockSpec((tf, d), lambda t, f, te, tv: (t, 0)),
                  pl.BlockSpec((None, d, tff), lambda t, f, te, tv: (te[t], 0, f)),
                  pl.BlockSpec((None, d, tff), lambda t, f, te, tv: (te[t], 0, f)),
                  pl.BlockSpec((None, tff, d), lambda t, f, te, tv: (te[t], f, 0))],
        out_specs=pl.BlockSpec((tf, d), lambda t, f, te, tv: (t, 0)),
        scratch_shapes=[pltpu.VMEM((tf, d), F32)],
    )
    return pl.pallas_call(
        _moe_ffn_kernel, grid_spec=grid_spec, out_shape=jax.ShapeDtypeStruct((r, d), BF16),
        compiler_params=_params("arbitrary", "arbitrary"), name="moe_ffn",
    )(te, tv, xs, wg, wu, wd)


def _moe_combine_kernel(t_ref, ck_ref, e_ref, flag_ref, goff_ref, ys_ref, rank_ref, comb_ref, x_ref, g_ref, b_ref,
                        o_ref, acc_ref, *, alpha):
    w = pl.program_id(0)
    flag = flag_ref[w]
    tf, tc = ys_ref.shape[0], x_ref.shape[0]

    @pl.when((flag & FLAG_FIRST) != 0)
    def _():
        acc_ref[...] = jnp.zeros_like(acc_ref)

    @pl.when((flag & FLAG_VALID) != 0)
    def _():
        e = e_ref[w]
        shift = (goff_ref[e] - t_ref[w] * tf).astype(F32)
        mine = lax.broadcasted_iota(jnp.int32, (tc, LANES), 1) == e
        dest = jnp.sum(jnp.where(mine, rank_ref[...], 0.0), axis=1, keepdims=True) + shift
        weight = jnp.sum(jnp.where(mine, comb_ref[...], 0.0), axis=1, keepdims=True)
        col = lax.broadcasted_iota(jnp.int32, (tc, tf), 1).astype(F32)
        onehot = jnp.where(col == dest, 1.0, 0.0).astype(BF16)
        acc_ref[...] += weight * _mm(onehot, ys_ref[...])

    @pl.when((flag & FLAG_LAST) != 0)
    def _():
        o_ref[...] = _ln(alpha * x_ref[...] + acc_ref[...], g_ref[...], b_ref[...])


def _moe_combine(plan, goff, ys, rank, comb, x, g, b, *, alpha):
    m, d = x.shape
    tf = tc = MOE_TILE
    ni = plan[0].shape[0]
    chunk = lambda width: pl.BlockSpec((tc, width), lambda w, t, ck, e, fl, go: (ck[w], 0))
    full = lambda a: pl.BlockSpec(a.shape, lambda w, t, ck, e, fl, go: (0,) * a.ndim)
    grid_spec = pltpu.PrefetchScalarGridSpec(
        num_scalar_prefetch=5, grid=(ni,),
        in_specs=[pl.BlockSpec((tf, d), lambda w, t, ck, e, fl, go: (t[w], 0)), chunk(LANES), chunk(LANES), chunk(d),
                  full(g), full(b)],
        out_specs=chunk(d),
        scratch_shapes=[pltpu.VMEM((tc, d), F32)],
    )
    return pl.pallas_call(
        functools.partial(_moe_combine_kernel, alpha=alpha), grid_spec=grid_spec,
        out_shape=jax.ShapeDtypeStruct((m, d), F32),
        compiler_params=_params("arbitrary"), name="moe_combine",
    )(*plan, goff, ys, rank, comb, x, g, b)


def _moe_routed(x, x16, wr, br, wg, wu, wd, g, b, *, alpha):
    m = x.shape[0]
    tf = MOE_TILE
    n_ck = m // tf
    nt = -(-(2 * m + N_EXPERTS * (tf - 1)) // tf)
    ni = nt + N_EXPERTS * n_ck
    comb, rank, rankt, cum = _route(x, wr, br)
    goff, te, tv, by_tile, by_chunk = _moe_plan(cum, n_ck=n_ck, nt=nt, ni=ni)
    xs = _moe_gather(by_tile, goff, x16, rankt, nt=nt)
    ys = _moe_ffn(te, tv, xs, wg, wu, wd)
    return _moe_combine(by_chunk, goff, ys, rank, comb, x, g, b, alpha=alpha)


def _ple_kernel(x_ref, p_ref, wg_ref, wp_ref, g_ref, b_ref, o_ref, *, alpha):
    x = x_ref[...]
    gate = _sigmoid(_mm(x, wg_ref[...]))
    o_ref[...] = _ln(alpha * x + gate * _mm(p_ref[...], wp_ref[...]), g_ref[...], b_ref[...])


def _ple(x, p, wg, wp, g, b, *, alpha):
    m, d = x.shape
    tm = _row_tile(m, 512)
    full = lambda a: pl.BlockSpec(a.shape, lambda i: (0,) * a.ndim)
    return pl.pallas_call(
        functools.partial(_ple_kernel, alpha=alpha),
        grid=(m // tm,),
        in_specs=[pl.BlockSpec((tm, d), lambda i: (i, 0)), pl.BlockSpec((tm, p.shape[1]), lambda i: (i, 0)),
                  full(wg), full(wp), full(g), full(b)],
        out_specs=pl.BlockSpec((tm, d), lambda i: (i, 0)), out_shape=jax.ShapeDtypeStruct((m, d), F32),
        compiler_params=_params("arbitrary"), name="ple",
    )(x, p, wg, wp, g, b)


def kernel(x_prompt, x_sample, p_prompt, p_sample, cache_k, cache_v, state_conv_b, state_conv_d, page_table, w_in, w_gate, b_gate, a_ln_g, a_ln_b, a_w_s, a_b_s, b_conv_w, d_conv_w, d_conv_b, d_ln_g, d_ln_b, w_branch, w_out, ln_g, ln_b, ffn_w_gate, ffn_w_up, ffn_w_down, moe_w_router, moe_b_router, moe_w_gate, moe_w_up, moe_w_down, ple_w_gate, ple_w_proj):
    depth = w_in.shape[0]
    alpha = (2 * depth) ** 0.25
    nbatch, seq, d = x_prompt.shape
    ns, dec_seq, _ = x_sample.shape
    assert dec_seq == 1, "the sample group decodes one token per sequence"
    n_pool, page = cache_k.shape[1], cache_k.shape[2]
    n_pages = page_table.shape[1]
    assert MOBA_BLOCK % page == 0 and (n_pages * page) % MOBA_BLOCK == 0 and (n_pages * page) % CHUNK == 0
    ppb = MOBA_BLOCK // page
    nbp = n_pages // ppb
    assert nbp >= MOBA_TOPK
    pt_flat = page_table.reshape(-1).astype(jnp.int32)
    ck_all = jnp.transpose(cache_k, (0, 1, 3, 4, 2)).reshape(depth, n_pool, BW, page)
    cv_all = jnp.transpose(cache_v, (0, 1, 3, 4, 2)).reshape(depth, n_pool, BW, page)
    row2 = lambda a: a.reshape(1, -1)

    xp = x_prompt
    xs = x_sample.reshape(ns, d)
    outs = {k: [] for k in ("kp", "vp", "ks", "vs", "cbp", "cbs", "cdp", "cds", "cvs")}
    for i in range(depth):
        w_in_i = w_in[i].astype(BF16)
        w_gate_i = w_gate[i].astype(BF16)
        w_branch_i = w_branch[i].astype(BF16)
        w_out_i = w_out[i].astype(BF16)
        alng, alnb, dlng, dlnb, dcb = (row2(a[i]) for a in (a_ln_g, a_ln_b, d_ln_g, d_ln_b, d_conv_b))
        bg = row2(b_gate[i])

        abias = jnp.repeat(a_b_s[i].T, HEAD_DIM, axis=1)
        ya, yb, yd, q, k, v, kb, vt, km, tb, td = _inmix_prompt(
            xp, w_in_i, alng, alnb, a_w_s[i], abias, b_conv_w[i], d_conv_w[i], dcb, dlng, dlnb)
        yc = _attn_prompt(q, kb, vt, km.reshape(nbatch, seq // MOBA_BLOCK, BW))
        outs["kp"].append(k.reshape(nbatch, seq, HEADS, HEAD_DIM))
        outs["vp"].append(v.reshape(nbatch, seq, HEADS, HEAD_DIM))
        outs["cbp"].append(tb)
        outs["cdp"].append(td)
        m = nbatch * seq
        x1p, x1p16 = _merge(xp.reshape(m, d), ya.reshape(m, BW), yb.reshape(m, BW), yc.reshape(m, BW), yd.reshape(m, BW),
                     w_gate_i, bg, w_branch_i, w_out_i, row2(ln_g[i, 0]), row2(ln_b[i, 0]), alpha=alpha)

        w00 = row2(jnp.repeat(a_w_s[i][:, 0, 0], HEAD_DIM))
        b0 = row2(jnp.repeat(a_b_s[i][:, 0], HEAD_DIM))
        hb = jnp.transpose(state_conv_b[i], (1, 0, 2))
        hd = jnp.transpose(state_conv_d[i], (1, 0, 2))
        sya, syb, syd, sq, sk, sv, svn, scbn, scdn = _inmix_sample(
            xs, w_in[i], alng, alnb, w00, b0, b_conv_w[i], hb, d_conv_w[i], dcb, hd, dlng, dlnb)
        syc = _attn_sample(pt_flat, ck_all, cv_all, sq, sk, sv, layer=i, n_pages=n_pages, ppb=ppb)
        outs["ks"].append(sk.reshape(ns, 1, HEADS, HEAD_DIM))
        outs["vs"].append(sv.reshape(ns, 1, HEADS, HEAD_DIM))
        outs["cbs"].append(jnp.concatenate([state_conv_b[i][:, 1:], scbn[:, None]], axis=1))
        outs["cds"].append(jnp.concatenate([state_conv_d[i][:, 1:], scdn[:, None]], axis=1))
        outs["cvs"].append(svn.reshape(ns, 1, BW))
        x1s, _ = _merge(xs, sya, syb, syc.reshape(ns, BW), syd,
                     w_gate[i], bg, w_branch[i], w_out[i], row2(ln_g[i, 0]), row2(ln_b[i, 0]), alpha=alpha)

        g1, b1 = row2(ln_g[i, 1]), row2(ln_b[i, 1])
        j = i // 2
        if i % 2 == 0:
            wg, wu, wd = (w[j].astype(BF16) for w in (ffn_w_gate, ffn_w_up, ffn_w_down))
            x2p = _ffn(x1p, wg, wu, wd, g1, b1, alpha=alpha)
            x2s = _ffn(x1s, ffn_w_gate[j], ffn_w_up[j], ffn_w_down[j], g1, b1, alpha=alpha)
        else:
            wg, wu, wd = (w[j].astype(BF16) for w in (moe_w_gate, moe_w_up, moe_w_down))
            ne = moe_w_router.shape[2]
            wr = jnp.pad(moe_w_router[j], ((0, 0), (0, LANES - ne)))
            br = row2(jnp.pad(moe_b_router[j], (0, LANES - ne), constant_values=NEG))
            x2p = _moe_routed(x1p, x1p16, wr, br, wg, wu, wd, g1, b1, alpha=alpha)
            x2s = _moe(x1s, _router(x1s, wr, br), wg, wu, wd, g1, b1, alpha=alpha)

        wpg = ple_w_gate[i].astype(BF16)
        wpp = ple_w_proj[i].astype(BF16)
        g2, b2 = row2(ln_g[i, 2]), row2(ln_b[i, 2])
        xp = _ple(x2p, p_prompt[i].reshape(m, -1), wpg, wpp, g2, b2, alpha=alpha).reshape(nbatch, seq, d)
        xs = _ple(x2s, p_sample[i].reshape(ns, -1), ple_w_gate[i], ple_w_proj[i], g2, b2, alpha=alpha)

    st = lambda key: jnp.stack(outs[key])
    return (xp, xs.reshape(ns, 1, d), st("kp"), st("vp"), st("ks"), st("vs"), st("cbp"), st("cbs"), st("cdp"),
            st("cds"), st("cvs"))
```
